```python
import jax, jax.numpy as jnp
from jax import lax
import numpy as np

D_MODEL = 4096
BATCH = 1
SEQ = 8192
DEPTH = 1

PLE_DIM = 256
HEAD_DIM = 128
D_MIX = D_MODEL
D_GDN = D_MIX // 2
D_ATT = D_MIX - D_GDN
GDN_V_HEADS = D_GDN // HEAD_DIM
GDN_K_HEADS = GDN_V_HEADS // 2
GDN_KEY_DIM = GDN_K_HEADS * HEAD_DIM
CONV_WIDTH = 5
CONV_DIM = 2 * GDN_KEY_DIM + D_GDN
GDN_CHUNK = 64
ATT_HEADS = D_ATT // HEAD_DIM
DILATED_PATTERNS = ((128, 1), (512, 4), (2048, 16))
ROPE_THETA = 10000.0
LN_EPS = 1e-5
RMS_EPS = 1e-6
NEG_INF = -1e30

SPLITS = (GDN_KEY_DIM, GDN_KEY_DIM, D_GDN, D_GDN, 2 * GDN_V_HEADS, 2 * GDN_V_HEADS,
          D_ATT, D_ATT, D_ATT, D_ATT)
N_IN = sum(SPLITS)
SPLIT_POINTS = tuple(int(c) for c in np.cumsum(SPLITS)[:-1])

kernel_name = 'hybrid_gdn_dilated_attention_encoder_layer'


def layer_norm(t, g, b):
    tf = t.astype(jnp.float32)
    mu = jnp.mean(tf, axis=-1, keepdims=True)
    var = jnp.mean(jnp.square(tf - mu), axis=-1, keepdims=True)
    return ((tf - mu) * lax.rsqrt(var + LN_EPS) * g + b).astype(t.dtype)


def l2_normalize(t):
    return t * lax.rsqrt(jnp.sum(jnp.square(t), axis=-1, keepdims=True) + RMS_EPS)


def centred_depthwise_conv(u, w):
    width, chans = w.shape
    pad = (width - 1) // 2
    return lax.conv_general_dilated(u, w[:, None, :], window_strides=(1,), padding=[(pad, pad)],
                                    dimension_numbers=('NWC', 'WIO', 'NWC'),
                                    feature_group_count=chans)


def rotary(t, pos):
    half = t.shape[-1] // 2
    inv_freq = 1.0 / (jnp.float32(ROPE_THETA) ** (jnp.arange(half, dtype=jnp.float32) / half))
    ang = pos.astype(jnp.float32)[:, None] * inv_freq[None, :]
    cos, sin = jnp.cos(ang)[:, None, :], jnp.sin(ang)[:, None, :]
    tf = t.astype(jnp.float32)
    t1, t2 = tf[..., :half], tf[..., half:]
    return jnp.concatenate([t1 * cos - t2 * sin, t1 * sin + t2 * cos], axis=-1).astype(t.dtype)


def gated_delta_chunked(q, k, v, g, beta):
    B, H, S, Dk = k.shape
    Dv = v.shape[-1]
    C = GDN_CHUNK
    N = S // C
    q, k, v = (t.reshape(B, H, N, C, t.shape[-1]) for t in (q, k, v))
    g = jnp.cumsum(g.reshape(B, H, N, C), axis=-1)
    beta = beta.reshape(B, H, N, C, 1)
    k_beta = k * beta
    v_beta = v * beta
    incl = jnp.tril(jnp.ones((C, C), bool))
    strict = jnp.tril(jnp.ones((C, C), bool), -1)
    decay = jnp.exp(jnp.where(incl, g[..., :, None] - g[..., None, :], -jnp.inf))
    a_mat = jnp.where(strict, jnp.einsum('bhnid,bhnjd->bhnij', k_beta, k) * decay, 0.0)
    eye = jnp.eye(C, dtype=jnp.float32)
    rhs = jnp.concatenate([v_beta, k_beta * jnp.exp(g)[..., None]], axis=-1)
    sol = lax.linalg.triangular_solve(eye + a_mat, rhs, left_side=True, lower=True,
                                      unit_diagonal=True)
    u, w = sol[..., :Dv], sol[..., Dv:]
    qk = jnp.where(incl, jnp.einsum('bhnid,bhnjd->bhnij', q, k) * decay, 0.0)
    g_last = g[..., -1]
    k_tail = k * jnp.exp(g_last[..., None] - g)[..., None]
    q_dec = q * jnp.exp(g)[..., None]

    def step(state, xs):
        u_c, w_c, qk_c, kt_c, qd_c, gl_c = xs
        v_new = u_c - jnp.einsum('bhck,bhkv->bhcv', w_c, state)
        o_c = (jnp.einsum('bhck,bhkv->bhcv', qd_c, state)
               + jnp.einsum('bhcj,bhjv->bhcv', qk_c, v_new))
        state = (state * jnp.exp(gl_c)[..., None, None]
                 + jnp.einsum('bhck,bhcv->bhkv', kt_c, v_new))
        return state, o_c

    xs = tuple(jnp.moveaxis(t, 2, 0) for t in (u, w, qk, k_tail, q_dec, g_last))
    state0 = jnp.zeros((B, H, Dk, Dv), jnp.float32)
    _, o = lax.scan(step, state0, xs)
    return jnp.moveaxis(o, 0, 2).reshape(B, H, S, Dv)


def gdn_branch(qa, ka, va, za, aa, ba, conv_w, a_log, dt_bias, norm_w):
    B, S, _ = qa.shape
    qkv = jax.nn.silu(centred_depthwise_conv(jnp.concatenate([qa, ka, va], axis=-1), conv_w))
    qkv = qkv.astype(jnp.float32)
    q = qkv[..., :GDN_KEY_DIM].reshape(B, S, GDN_K_HEADS, HEAD_DIM)
    k = qkv[..., GDN_KEY_DIM:2 * GDN_KEY_DIM].reshape(B, S, GDN_K_HEADS, HEAD_DIM)
    v = qkv[..., 2 * GDN_KEY_DIM:].reshape(B, S, GDN_V_HEADS, HEAD_DIM)
    rep = GDN_V_HEADS // GDN_K_HEADS
    q = jnp.repeat(l2_normalize(q) * (HEAD_DIM ** -0.5), rep, axis=2).transpose(0, 2, 1, 3)
    k = jnp.repeat(l2_normalize(k), rep, axis=2).transpose(0, 2, 1, 3)
    v = v.transpose(0, 2, 1, 3)
    a = aa.astype(jnp.float32).reshape(B, S, 2, GDN_V_HEADS)
    b = ba.astype(jnp.float32).reshape(B, S, 2, GDN_V_HEADS)
    g = -jnp.exp(a_log.astype(jnp.float32)) * jax.nn.softplus(a + dt_bias.astype(jnp.float32))
    beta = jax.nn.sigmoid(b)
    g = g.transpose(2, 0, 3, 1)
    beta = beta.transpose(2, 0, 3, 1)
    flip = lambda t: jnp.flip(t, axis=2)
    o_fwd = gated_delta_chunked(q, k, v, g[0], beta[0])
    o_bwd = flip(gated_delta_chunked(flip(q), flip(k), flip(v), flip(g[1]), flip(beta[1])))
    o = (o_fwd + o_bwd).transpose(0, 2, 1, 3)
    o = o * lax.rsqrt(jnp.mean(jnp.square(o), axis=-1, keepdims=True) + RMS_EPS) * norm_w
    return (o.reshape(B, S, D_GDN) * jax.nn.silu(za.astype(jnp.float32))).astype(qa.dtype)


def banded_attention(q, k, v, radius):
    B, L, G, H, D = q.shape
    R = radius
    nb = -(-L // R)
    Lp = nb * R
    pad_q = Lp - L
    qp = jnp.pad(q, ((0, 0), (0, pad_q), (0, 0), (0, 0), (0, 0)))
    kp = jnp.pad(k, ((0, 0), (R, pad_q + R), (0, 0), (0, 0), (0, 0)))
    vp = jnp.pad(v, ((0, 0), (R, pad_q + R), (0, 0), (0, 0), (0, 0)))
    qb = qp.reshape(B, nb, R, G, H, D)

    def key_blocks(t):
        return jnp.concatenate([t[:, j * R:j * R + Lp].reshape(B, nb, R, G, H, D)
                                for j in range(3)], axis=2)

    kb, vb = key_blocks(kp), key_blocks(vp)
    s = jnp.einsum('bnqghd,bnkghd->bnghqk', qb, kb).astype(jnp.float32)
    blk = jnp.arange(nb)[:, None] * R
    qpos = blk + jnp.arange(R)[None, :]
    kpos = blk - R + jnp.arange(3 * R)[None, :]
    valid = ((jnp.abs(qpos[:, :, None] - kpos[:, None, :]) <= R)
             & (kpos[:, None, :] >= 0) & (kpos[:, None, :] < L))
    s = jnp.where(valid[None, :, None, None], s, NEG_INF)
    m = jnp.max(s, axis=-1, keepdims=True)
    e = jnp.exp(s - m)
    den = jnp.sum(e, axis=-1, keepdims=True)
    o = jnp.einsum('bnghqk,bnkghd->bnqghd', (e / den).astype(v.dtype), vb)
    lse = (m + jnp.log(den))[..., 0].transpose(0, 1, 4, 2, 3)
    o = o.reshape(B, Lp, G, H, D)[:, :L]
    lse = lse.reshape(B, Lp, G, H)[:, :L]
    return o, lse


def dilated_window_attention(q, k, v, dilation, radius):
    B, S, H, D = q.shape
    split = lambda t: t.reshape(B, S // dilation, dilation, H, D)
    o, lse = banded_attention(split(q), split(k), split(v), radius)
    return o.reshape(B, S, H, D), lse.reshape(B, S, H)


def dilated_attention_branch(qb, kb, vb, zb):
    B, S, _ = qb.shape
    pos = jnp.arange(S)
    q = rotary(qb.reshape(B, S, ATT_HEADS, HEAD_DIM), pos) * (HEAD_DIM ** -0.5)
    k = rotary(kb.reshape(B, S, ATT_HEADS, HEAD_DIM), pos)
    v = vb.reshape(B, S, ATT_HEADS, HEAD_DIM)
    outs, lses = [], []
    for window, dilation in DILATED_PATTERNS:
        o, lse = dilated_window_attention(q, k, v, dilation, window // (2 * dilation))
        outs.append(o)
        lses.append(lse)
    wts = jax.nn.softmax(jnp.stack(lses), axis=0)
    o = jnp.einsum('pbsh,pbshd->bshd', wts, jnp.stack(outs).astype(jnp.float32))
    return (o.reshape(B, S, D_ATT) * jax.nn.silu(zb.astype(jnp.float32))).astype(qb.dtype)


def hybrid_mixer(h, w_in, conv_w, a_log, dt_bias, gdn_norm_w, w_out):
    proj = jnp.einsum('bsd,dn->bsn', h, w_in)
    qa, ka, va, za, aa, ba, qb, kb, vb, zb = jnp.split(proj, SPLIT_POINTS, axis=-1)
    y_gdn = gdn_branch(qa, ka, va, za, aa, ba, conv_w, a_log, dt_bias, gdn_norm_w)
    y_att = dilated_attention_branch(qb, kb, vb, zb)
    y = jnp.concatenate([y_gdn, y_att], axis=-1)
    return jnp.einsum('bsm,md->bsd', y, w_out)


def setup_inputs(seed: int = 0) -> dict:
    key = jax.random.key(seed)
    ks = jax.random.split(key, 16)
    beta_init = (8 * DEPTH) ** -0.25
    offs = np.concatenate([[0], np.cumsum(SPLITS)])
    col_scale = np.ones((N_IN,), np.float32)
    col_scale[offs[2]:offs[3]] = beta_init
    col_scale[offs[8]:offs[9]] = beta_init
    nrm = jax.random.normal
    x = nrm(ks[0], (BATCH, SEQ, D_MODEL), jnp.float32)
    p = nrm(ks[1], (DEPTH, BATCH, SEQ, PLE_DIM), jnp.float32)
    w_in = nrm(ks[2], (DEPTH, D_MODEL, N_IN), jnp.float32) * (D_MODEL ** -0.5) * jnp.asarray(col_scale)
    conv_w = nrm(ks[3], (DEPTH, CONV_WIDTH, CONV_DIM), jnp.float32) * (CONV_WIDTH ** -0.5)
    a_log = jnp.log(jax.random.uniform(ks[4], (DEPTH, 2, GDN_V_HEADS), jnp.float32, 1.0, 16.0))
    dt = jnp.exp(jax.random.uniform(ks[5], (DEPTH, 2, GDN_V_HEADS), jnp.float32,
                                    float(np.log(1e-3)), float(np.log(1e-1))))
    dt_bias = dt + jnp.log(-jnp.expm1(-dt))
    gdn_norm_w = 1.0 + 0.02 * nrm(ks[6], (DEPTH, HEAD_DIM), jnp.float32)
    w_out = nrm(ks[7], (DEPTH, D_MIX, D_MODEL), jnp.float32) * (D_MIX ** -0.5) * beta_init
    ln1_g = 1.0 + 0.02 * nrm(ks[8], (DEPTH, D_MODEL), jnp.float32)
    ln1_b = 0.02 * nrm(ks[9], (DEPTH, D_MODEL), jnp.float32)
    w_ple_gate = nrm(ks[10], (DEPTH, D_MODEL, D_MODEL), jnp.float32) * (D_MODEL ** -0.5)
    w_ple_up = nrm(ks[11], (DEPTH, PLE_DIM, D_MODEL), jnp.float32) * (PLE_DIM ** -0.5) * beta_init
    ln2_g = 1.0 + 0.02 * nrm(ks[12], (DEPTH, D_MODEL), jnp.float32)
    ln2_b = 0.02 * nrm(ks[13], (DEPTH, D_MODEL), jnp.float32)
    return {'x': x, 'p': p, 'w_in': w_in, 'conv_w': conv_w, 'a_log': a_log,
            'dt_bias': dt_bias, 'gdn_norm_w': gdn_norm_w, 'w_out': w_out,
            'ln1_g': ln1_g, 'ln1_b': ln1_b, 'w_ple_gate': w_ple_gate, 'w_ple_up': w_ple_up,
            'ln2_g': ln2_g, 'ln2_b': ln2_b}


def reference(x, p, w_in, conv_w, a_log, dt_bias, gdn_norm_w, w_out, ln1_g, ln1_b,
              w_ple_gate, w_ple_up, ln2_g, ln2_b):
    alpha = (2 * DEPTH) ** 0.25
    h = x
    for i in range(DEPTH):
        mix = hybrid_mixer(h, w_in[i], conv_w[i], a_log[i], dt_bias[i], gdn_norm_w[i], w_out[i])
        h = layer_norm(alpha * h + mix, ln1_g[i], ln1_b[i])
        gate = jax.nn.sigmoid(jnp.einsum('bsd,de->bse', h, w_ple_gate[i]).astype(jnp.float32))
        ple = (gate * jnp.einsum('bsk,kd->bsd', p[i], w_ple_up[i]).astype(jnp.float32)).astype(h.dtype)
        h = layer_norm(alpha * h + ple, ln2_g[i], ln2_b[i])
    return h
```

```python
import functools

import jax
import jax.numpy as jnp
from jax import lax
from jax.experimental import pallas as pl
from jax.experimental.pallas import tpu as pltpu

HEAD_DIM = 128
GDN_V_HEADS = 16
GDN_K_HEADS = 8
ATT_HEADS = 16
CONV_WIDTH = 5
GDN_CHUNK = 64
DILATIONS = (16, 4, 1)
ATT_RADIUS = 64
ROPE_THETA = 10000.0
LN_EPS = 1e-5
RMS_EPS = 1e-6
NEG_INF = -1e30
Q_SCALE = HEAD_DIM ** -0.5

LANES = 128
SUBLANES = 8
VMEM_LIMIT = 56 * 1024 * 1024

F32 = jnp.float32
BF16 = jnp.bfloat16
HI = lax.Precision.HIGHEST


def _params(sem, vmem=VMEM_LIMIT):
    return pltpu.CompilerParams(dimension_semantics=sem, vmem_limit_bytes=vmem)


def _sigmoid(x):
    return 1.0 / (1.0 + jnp.exp(-x))


def _silu(x):
    return x * _sigmoid(x)


def _dot(a, b):
    return jnp.dot(a.astype(BF16), b.astype(BF16), preferred_element_type=F32)


def _dot_nt(a, b):
    return lax.dot_general(a.astype(BF16), b.astype(BF16), (((1,), (1,)), ((), ())),
                           preferred_element_type=F32)


def _dot_tn(a, b):
    return lax.dot_general(a.astype(BF16), b.astype(BF16), (((0,), (0,)), ((), ())),
                           preferred_element_type=F32)


def _dot_hi(a, b):
    return jnp.dot(a, b, preferred_element_type=F32, precision=HI)


def _epi_none(acc, j):
    return acc


def _epi_silu(acc, j):
    return _silu(acc)


def _epi_rotary(acc, j, cos_ref, sin_ref, *, n_q_blocks):
    cos = cos_ref[...]
    sin = sin_ref[...]
    scale = jnp.where(j < n_q_blocks, Q_SCALE, 1.0).astype(F32)
    outs = []
    for c in range(acc.shape[1] // HEAD_DIM):
        t = acc[:, c * HEAD_DIM:(c + 1) * HEAD_DIM]
        outs.append((t * cos + pltpu.roll(t, HEAD_DIM // 2, axis=1) * sin) * scale)
    return jnp.concatenate(outs, axis=1)


def _epi_gates(acc, j, alog_ref, dtb_ref):
    t = acc + dtb_ref[...]
    softplus = jnp.maximum(t, 0.0) + jnp.log1p(jnp.exp(-jnp.abs(t)))
    g = -jnp.exp(alog_ref[...]) * softplus
    beta = _sigmoid(acc)
    lane = lax.broadcasted_iota(jnp.int32, acc.shape, 1)
    n_gate = 2 * GDN_V_HEADS
    return jnp.where(lane < n_gate, g, jnp.where(lane < 2 * n_gate, beta, 0.0))


def _mm_kernel(x_ref, w_ref, *refs, epilogue, nk):
    o_ref, acc_ref = refs[-2], refs[-1]
    extra = refs[:-2]
    j = pl.program_id(1)
    k = pl.program_id(2)
    prod = jnp.dot(x_ref[...], w_ref[...], preferred_element_type=F32)

    @pl.when(k == 0)
    def _():
        acc_ref[...] = prod

    @pl.when(k > 0)
    def _():
        acc_ref[...] += prod

    @pl.when(k == nk - 1)
    def _():
        o_ref[...] = epilogue(acc_ref[...], j, *extra).astype(o_ref.dtype)


def _project(x, w, name, epilogue=_epi_none, extra=(), extra_specs=(), out_dtype=F32, tm=1024, tn=1024, tk=512):
    m, kd = x.shape
    n = w.shape[1]
    tm, tn, tk = min(tm, m), min(tn, n), min(tk, kd)
    assert m % tm == 0 and n % tn == 0 and kd % tk == 0
    nk = kd // tk
    return pl.pallas_call(
        functools.partial(_mm_kernel, epilogue=epilogue, nk=nk),
        grid=(m // tm, n // tn, nk),
        in_specs=[pl.BlockSpec((tm, tk), lambda i, j, k: (i, k)),
                  pl.BlockSpec((tk, tn), lambda i, j, k: (k, j)), *extra_specs],
        out_specs=pl.BlockSpec((tm, tn), lambda i, j, k: (i, j)),
        out_shape=jax.ShapeDtypeStruct((m, n), out_dtype),
        scratch_shapes=[pltpu.VMEM((tm, tn), F32)],
        compiler_params=_params(("parallel", "parallel", "arbitrary")),
        name=name,
    )(x, w, *extra)


def _conv_kernel(u_ref, w_ref, o_ref, *, seq, tile, n_q, n_k):
    j = pl.program_id(0)
    n_tiles = seq // tile
    w = w_ref[...]
    pad = (CONV_WIDTH - 1) // 2
    ext_rows = tile + 2 * SUBLANES

    def body(i, carry):
        t0 = pl.multiple_of(i * tile, tile)
        cur = u_ref[pl.ds(t0, tile), :]
        p0 = pl.multiple_of(jnp.maximum(t0 - SUBLANES, 0), SUBLANES)
        n0 = pl.multiple_of(jnp.minimum(t0 + tile, seq - SUBLANES), SUBLANES)
        prev = jnp.where(i > 0, u_ref[pl.ds(p0, SUBLANES), :], 0.0)
        nxt = jnp.where(i < n_tiles - 1, u_ref[pl.ds(n0, SUBLANES), :], 0.0)
        ext = jnp.concatenate([prev, cur, nxt], axis=0)
        acc = jnp.zeros((tile, LANES), F32)
        for tap in range(CONV_WIDTH):
            shift = (pad - tap) % ext_rows
            sh = ext if shift == 0 else pltpu.roll(ext, shift, axis=0)
            acc = acc + sh[SUBLANES:SUBLANES + tile] * w[tap:tap + 1, :]
        y = _silu(acc)
        inv = lax.rsqrt(jnp.sum(y * y, axis=-1, keepdims=True) + RMS_EPS)
        scale = jnp.where(j < n_q, inv * Q_SCALE, jnp.where(j < n_q + n_k, inv, 1.0))
        o_ref[pl.ds(t0, tile), :] = y * scale
        return carry

    lax.fori_loop(0, n_tiles, body, 0)


def _conv_norm(u, conv_w8):
    seq, chans = u.shape
    tile = min(256, seq)
    return pl.pallas_call(
        functools.partial(_conv_kernel, seq=seq, tile=tile, n_q=GDN_K_HEADS, n_k=GDN_K_HEADS),
        grid=(chans // LANES,),
        in_specs=[pl.BlockSpec((seq, LANES), lambda j: (0, j)),
                  pl.BlockSpec((SUBLANES, LANES), lambda j: (0, j))],
        out_specs=pl.BlockSpec((seq, LANES), lambda j: (0, j)),
        out_shape=jax.ShapeDtypeStruct((seq, chans), F32),
        compiler_params=_params(("parallel",)),
        name="gdn_conv_norm",
    )(u, conv_w8)


GDN_SUB = 16


def _unit_tri_inverse(a, same_block, eye):
    d = jnp.where(same_block, a, 0.0)
    off = a - d
    d2 = _dot_hi(d, d)
    d4 = _dot_hi(d2, d2)
    d8 = _dot_hi(d4, d4)
    p = eye - d
    p = p + _dot_hi(p, d2)
    p = p + _dot_hi(p, d4)
    p = p + _dot_hi(p, d8)
    m = _dot_hi(p, off)
    m2 = _dot_hi(m, m)
    q = eye - m
    q = q + _dot_hi(q, m2)
    return _dot_hi(q, p)


def _gdn_kernel(qf_ref, kf_ref, vf_ref, gcf_ref, grf_ref, qb_ref, kb_ref, vb_ref, gcb_ref, grb_ref,
                of_ref, ob_ref, st_ref, *, n_chunks, chunk):
    @pl.when(pl.program_id(1) == 0)
    def _():
        st_ref[...] = jnp.zeros_like(st_ref)

    row = lax.broadcasted_iota(jnp.int32, (chunk, chunk), 0)
    col = lax.broadcasted_iota(jnp.int32, (chunk, chunk), 1)
    eye = (row == col).astype(F32)
    same_block = (row // GDN_SUB) == (col // GDN_SUB)
    lower = (col <= row)
    lower_f = lower.astype(F32)
    upper_f = (col >= row).astype(F32)

    dirs = ((qf_ref, kf_ref, vf_ref, gcf_ref, grf_ref, of_ref), (qb_ref, kb_ref, vb_ref, gcb_ref, grb_ref, ob_ref))
    for d, (q_ref, k_ref, v_ref, gc_ref, gr_ref, o_ref) in enumerate(dirs):
        backward = d == 1
        incl = (col >= row) if backward else lower
        strict = (col > row) if backward else (col < row)
        tri_c = upper_f if backward else lower_f
        tri_r = lower_f if backward else upper_f
        last = 0 if backward else chunk - 1
        order = range(n_chunks - 1, -1, -1) if backward else range(n_chunks)
        for c in order:
            rows = slice(c * chunk, (c + 1) * chunk)
            q = q_ref[rows, :]
            k = k_ref[rows, :]
            gb_col = gc_ref[0, rows, :]
            gb_row = gr_ref[0, c]
            cum_col = _dot_hi(tri_c, gb_col)
            cum_row = _dot_hi(gb_row, tri_r)
            kk = _dot_nt(k, k)
            qk = _dot_nt(q, k)
            for e in range(2):
                v = v_ref[rows, e * HEAD_DIM:(e + 1) * HEAD_DIM]
                state = st_ref[2 * d + e]
                g_col = cum_col[:, e:e + 1]
                g_row = cum_row[e:e + 1, :]
                beta = gb_col[:, 2 + e:3 + e]
                g_last = g_col[last:last + 1, :]
                decay = jnp.where(incl, jnp.exp(jnp.where(incl, g_col - g_row, 0.0)), 0.0)
                a = jnp.where(strict, kk * decay, 0.0) * beta
                t = _unit_tri_inverse(a, same_block, eye)
                eg = jnp.exp(g_col)
                u = _dot_hi(t, v * beta)
                w = _dot_hi(t, k * (beta * eg))
                v_new = u - _dot(w, state)
                o = _dot(q * eg, state) + _dot(qk * decay, v_new)
                k_tail = k * jnp.exp(g_last - g_col)
                st_ref[2 * d + e] = state * jnp.exp(g_last) + _dot_tn(k_tail, v_new)
                o_ref[rows, e * HEAD_DIM:(e + 1) * HEAD_DIM] = o


def _gated_delta(qkv, g_cols, g_rows, n_chunks=2):
    seq = qkv.shape[0]
    blk = n_chunks * GDN_CHUNK
    nb = seq // blk
    rep = GDN_V_HEADS // GDN_K_HEADS
    vw = rep * HEAD_DIM
    v_off = 2 * GDN_K_HEADS * HEAD_DIM // vw

    def specs(rowmap):
        return [pl.BlockSpec((blk, HEAD_DIM), lambda j, n: (rowmap(n), j)),
                pl.BlockSpec((blk, HEAD_DIM), lambda j, n: (rowmap(n), GDN_K_HEADS + j)),
                pl.BlockSpec((blk, vw), lambda j, n: (rowmap(n), v_off + j)),
                pl.BlockSpec((1, blk, SUBLANES), lambda j, n: (j, rowmap(n), 0)),
                pl.BlockSpec((1, n_chunks, SUBLANES, GDN_CHUNK), lambda j, n: (j, rowmap(n), 0, 0))]

    fwd = lambda n: n
    bwd = lambda n: nb - 1 - n
    out_sd = jax.ShapeDtypeStruct((seq, GDN_V_HEADS * HEAD_DIM), F32)
    return pl.pallas_call(
        functools.partial(_gdn_kernel, n_chunks=n_chunks, chunk=GDN_CHUNK),
        grid=(GDN_K_HEADS, nb),
        in_specs=specs(fwd) + specs(bwd),
        out_specs=[pl.BlockSpec((blk, vw), lambda j, n: (fwd(n), j)),
                   pl.BlockSpec((blk, vw), lambda j, n: (bwd(n), j))],
        out_shape=[out_sd, out_sd],
        scratch_shapes=[pltpu.VMEM((2 * rep, HEAD_DIM, HEAD_DIM), F32)],
        compiler_params=_params(("parallel", "arbitrary")),
        name="gdn_delta_rule",
    )(qkv, qkv, qkv, g_cols[0], g_rows[0], qkv, qkv, qkv, g_cols[1], g_rows[1])


def _att_kernel(q_ref, k_ref, v_ref, z_ref, o_ref, acc_ref, ml_ref, *, seq):
    lane = lax.broadcasted_iota(jnp.int32, (1, LANES), 1)
    for p, dil in enumerate(DILATIONS):
        length = seq // dil
        tq = min(256, length)
        win = min(tq + 2 * ATT_RADIUS, length)
        tiles = length // tq
        first, final = p == 0, p == len(DILATIONS) - 1

        def body(it, carry, dil=dil, tq=tq, win=win, tiles=tiles, length=length, first=first, final=final):
            r = it // tiles
            m0 = (it % tiles) * tq
            k0 = jnp.clip(m0 - ATT_RADIUS, 0, length - win)
            if dil == 1:
                qsl = pl.ds(pl.multiple_of(m0, tq), tq)
                ksl = pl.ds(pl.multiple_of(k0, ATT_RADIUS), win)
            else:
                qsl = pl.ds(r + dil * m0, tq, stride=dil)
                ksl = pl.ds(r + dil * k0, win, stride=dil)
            q = q_ref[qsl, :]
            k = k_ref[ksl, :]
            v = v_ref[ksl, :]
            s = _dot_nt(q, k)
            qpos = m0 + lax.broadcasted_iota(jnp.int32, (tq, win), 0)
            kpos = k0 + lax.broadcasted_iota(jnp.int32, (tq, win), 1)
            s = jnp.where(jnp.abs(qpos - kpos) <= ATT_RADIUS, s, NEG_INF)
            m_cur = jnp.max(s, axis=-1, keepdims=True)
            if first:
                m_new = m_cur
            else:
                ml = ml_ref[qsl, :]
                m_old, l_old = ml[:, 0:1], ml[:, LANES // 2:LANES // 2 + 1]
                m_new = jnp.maximum(m_old, m_cur)
            e = jnp.exp(s - m_new)
            l_new = jnp.sum(e, axis=-1, keepdims=True)
            acc = _dot(e, v)
            if not first:
                corr = jnp.exp(m_old - m_new)
                l_new = l_new + corr * l_old
                acc = acc + corr * acc_ref[qsl, :]
            if final:
                o_ref[qsl, :] = (acc / l_new * _silu(z_ref[qsl, :])).astype(o_ref.dtype)
            else:
                acc_ref[qsl, :] = acc
                ml_ref[qsl, :] = jnp.where(lane < LANES // 2, m_new, l_new)
            return carry

        lax.fori_loop(0, dil * tiles, body, 0)


def _dilated_attention(qk, v, z):
    seq, width = v.shape
    heads = width // HEAD_DIM
    col = pl.BlockSpec((seq, HEAD_DIM), lambda h: (0, h))
    return pl.pallas_call(
        functools.partial(_att_kernel, seq=seq),
        grid=(heads,),
        in_specs=[col, pl.BlockSpec((seq, HEAD_DIM), lambda h: (0, heads + h)), col, col],
        out_specs=col,
        out_shape=jax.ShapeDtypeStruct((seq, width), BF16),
        scratch_shapes=[pltpu.VMEM((seq, HEAD_DIM), F32), pltpu.VMEM((seq, LANES), F32)],
        compiler_params=_params(("parallel",)),
        name="dilated_attention",
    )(qk, qk, v, z)


def _gdn_out_kernel(of_ref, ob_ref, z_ref, w_ref, y_ref):
    w = w_ref[...]
    for h in range(of_ref.shape[1] // HEAD_DIM):
        sl = slice(h * HEAD_DIM, (h + 1) * HEAD_DIM)
        o = of_ref[:, sl] + ob_ref[:, sl]
        inv = lax.rsqrt(jnp.mean(o * o, axis=-1, keepdims=True) + RMS_EPS)
        y_ref[:, sl] = (o * inv * w * z_ref[:, sl]).astype(y_ref.dtype)


def _gdn_out(o_f, o_b, z_silu, norm_w):
    seq, width = o_f.shape
    tm = min(512, seq)
    blk = pl.BlockSpec((tm, width), lambda i: (i, 0))
    return pl.pallas_call(
        _gdn_out_kernel,
        grid=(seq // tm,),
        in_specs=[blk, blk, blk, pl.BlockSpec((1, HEAD_DIM), lambda i: (0, 0))],
        out_specs=blk,
        out_shape=jax.ShapeDtypeStruct((seq, width), BF16),
        compiler_params=_params(("parallel",)),
        name="gdn_norm_gate",
    )(o_f, o_b, z_silu, norm_w)


def _layer_norm_rows(pre_ref, g_ref, b_ref, out_refs, n_col, tn):
    width = n_col * tn
    total = jnp.zeros((pre_ref.shape[1], 1), F32)
    for c in range(n_col):
        total = total + jnp.sum(pre_ref[c], axis=-1, keepdims=True)
    mu = total / width
    sq = jnp.zeros_like(total)
    for c in range(n_col):
        dev = pre_ref[c] - mu
        sq = sq + jnp.sum(dev * dev, axis=-1, keepdims=True)
    inv = lax.rsqrt(sq / width + LN_EPS)
    for c in range(n_col):
        sl = slice(c * tn, (c + 1) * tn)
        y = (pre_ref[c] - mu) * inv * g_ref[:, sl] + b_ref[:, sl]
        for ref in out_refs:
            ref[:, sl] = y.astype(ref.dtype)


def _outproj_kernel(ya_ref, yb_ref, wa_ref, wb_ref, x_ref, g_ref, b_ref, h_ref, hb_ref, pre_ref, *,
                    n_col, tn, alpha):
    n = pl.program_id(1)
    mix = (jnp.dot(ya_ref[...], wa_ref[...], preferred_element_type=F32)
           + jnp.dot(yb_ref[...], wb_ref[...], preferred_element_type=F32))
    pre_ref[n] = alpha * x_ref[...] + mix

    @pl.when(n == n_col - 1)
    def _():
        _layer_norm_rows(pre_ref, g_ref, b_ref, (h_ref, hb_ref), n_col, tn)


def _out_projection(y_a, y_b, w_a, w_b, x, gamma, beta, alpha, tm=256, tn=512):
    seq, d_model = x.shape
    tm = min(tm, seq)
    n_col = d_model // tn
    ka, kb = y_a.shape[1], y_b.shape[1]
    row = lambda i, n: (i, 0)
    full = pl.BlockSpec((tm, d_model), row)
    vec = pl.BlockSpec((1, d_model), lambda i, n: (0, 0))
    return pl.pallas_call(
        functools.partial(_outproj_kernel, n_col=n_col, tn=tn, alpha=alpha),
        grid=(seq // tm, n_col),
        in_specs=[pl.BlockSpec((tm, ka), row), pl.BlockSpec((tm, kb), row),
                  pl.BlockSpec((ka, tn), lambda i, n: (0, n)), pl.BlockSpec((kb, tn), lambda i, n: (0, n)),
                  pl.BlockSpec((tm, tn), lambda i, n: (i, n)), vec, vec],
        out_specs=[full, full],
        out_shape=[jax.ShapeDtypeStruct((seq, d_model), F32), jax.ShapeDtypeStruct((seq, d_model), BF16)],
        scratch_shapes=[pltpu.VMEM((n_col, tm, tn), F32)],
        compiler_params=_params(("parallel", "arbitrary")),
        name="out_proj_ln",
    )(y_a, y_b, w_a, w_b, x, gamma, beta)


def _ple_kernel(hb_ref, wg_ref, p_ref, wu_ref, h_ref, g_ref, b_ref, o_ref, pre_ref, *, n_col, tn, alpha):
    n = pl.program_id(1)
    gate = _sigmoid(jnp.dot(hb_ref[...], wg_ref[...], preferred_element_type=F32))
    up = jnp.dot(p_ref[...], wu_ref[...], preferred_element_type=F32)
    pre_ref[n] = alpha * h_ref[...] + gate * up

    @pl.when(n == n_col - 1)
    def _():
        _layer_norm_rows(pre_ref, g_ref, b_ref, (o_ref,), n_col, tn)


def _ple_layer(h_bf, w_gate, p_bf, w_up, h, gamma, beta, alpha, tm=256, tn=512):
    seq, d_model = h.shape
    tm = min(tm, seq)
    n_col = d_model // tn
    pd = p_bf.shape[1]
    row = lambda i, n: (i, 0)
    vec = pl.BlockSpec((1, d_model), lambda i, n: (0, 0))
    return pl.pallas_call(
        functools.partial(_ple_kernel, n_col=n_col, tn=tn, alpha=alpha),
        grid=(seq // tm, n_col),
        in_specs=[pl.BlockSpec((tm, d_model), row), pl.BlockSpec((d_model, tn), lambda i, n: (0, n)),
                  pl.BlockSpec((tm, pd), row), pl.BlockSpec((pd, tn), lambda i, n: (0, n)),
                  pl.BlockSpec((tm, tn), lambda i, n: (i, n)), vec, vec],
        out_specs=pl.BlockSpec((tm, d_model), row),
        out_shape=jax.ShapeDtypeStruct((seq, d_model), F32),
        scratch_shapes=[pltpu.VMEM((n_col, tm, tn), F32)],
        compiler_params=_params(("parallel", "arbitrary")),
        name="ple_ln",
    )(h_bf, w_gate, p_bf, w_up, h, gamma, beta)


def _rotary_tables(seq):
    half = HEAD_DIM // 2
    inv_freq = 1.0 / (jnp.float32(ROPE_THETA) ** (jnp.arange(half, dtype=F32) / half))
    ang = jnp.arange(seq).astype(F32)[:, None] * inv_freq[None, :]
    cos, sin = jnp.cos(ang), jnp.sin(ang)
    return jnp.concatenate([cos, cos], axis=-1), jnp.concatenate([-sin, sin], axis=-1)


def _layer(h, p, w_in, conv_w, a_log, dt_bias, gdn_norm_w, w_out, ln1_g, ln1_b, w_ple_gate, w_ple_up,
           ln2_g, ln2_b, alpha):
    seq, d_model = h.shape
    key_dim = GDN_K_HEADS * HEAD_DIM
    d_gdn = GDN_V_HEADS * HEAD_DIM
    d_att = ATT_HEADS * HEAD_DIM
    n_gate = 2 * GDN_V_HEADS
    o_qkv = 0
    o_za = 2 * key_dim + d_gdn
    o_ab = o_za + d_gdn
    o_qk = o_ab + 2 * n_gate
    o_vb = o_qk + 2 * d_att
    o_zb = o_vb + d_att

    x_bf = h.astype(BF16)
    w = lambda lo, hi: w_in[:, lo:hi].astype(BF16)

    qkv_a = _project(x_bf, w(o_qkv, o_za), "proj_gdn_qkv")
    za = _project(x_bf, w(o_za, o_ab), "proj_gdn_gate", _epi_silu)
    w_ab = jnp.pad(w(o_ab, o_qk), ((0, 0), (0, LANES - 2 * n_gate)))
    pad_vec = lambda t: jnp.pad(t.reshape(1, n_gate).astype(F32), ((0, 0), (0, LANES - n_gate)))
    vec_spec = pl.BlockSpec((1, LANES), lambda i, j, k: (0, 0))
    gates = _project(x_bf, w_ab, "proj_gdn_decay", _epi_gates, extra=(pad_vec(a_log), pad_vec(dt_bias)),
                     extra_specs=(vec_spec, vec_spec))
    cos, sin = _rotary_tables(seq)
    tm_rot = min(1024, seq)
    tab_spec = pl.BlockSpec((tm_rot, HEAD_DIM), lambda i, j, k: (i, 0))
    tn_rot = 1024
    qk_b = _project(x_bf, w(o_qk, o_vb), "proj_att_qk",
                    functools.partial(_epi_rotary, n_q_blocks=d_att // tn_rot),
                    extra=(cos, sin), extra_specs=(tab_spec, tab_spec), tm=tm_rot, tn=tn_rot)
    v_b = _project(x_bf, w(o_vb, o_zb), "proj_att_v")
    z_b = _project(x_bf, w(o_zb, o_zb + d_att), "proj_att_gate")

    conv_w8 = jnp.pad(conv_w.astype(F32), ((0, SUBLANES - CONV_WIDTH), (0, 0)))
    qkv_n = _conv_norm(qkv_a, conv_w8)

    rep = GDN_V_HEADS // GDN_K_HEADS
    gg = gates[:, :n_gate].reshape(seq, 2, GDN_K_HEADS, rep)
    bb = gates[:, n_gate:2 * n_gate].reshape(seq, 2, GDN_K_HEADS, rep)
    gb = jnp.concatenate([gg, bb, jnp.zeros((seq, 2, GDN_K_HEADS, SUBLANES - 2 * rep), F32)], axis=-1)
    g_cols = gb.transpose(1, 2, 0, 3)
    g_rows = gb.reshape(seq // GDN_CHUNK, GDN_CHUNK, 2, GDN_K_HEADS, SUBLANES).transpose(2, 3, 0, 4, 1)
    o_f, o_b = _gated_delta(qkv_n, g_cols, g_rows)

    y_gdn = _gdn_out(o_f, o_b, za, gdn_norm_w.reshape(1, HEAD_DIM).astype(F32))

    y_att = _dilated_attention(qk_b, v_b, z_b)

    w_o = w_out.astype(BF16)
    h1, h1_bf = _out_projection(y_gdn, y_att, w_o[:d_gdn], w_o[d_gdn:], h,
                                ln1_g.reshape(1, d_model), ln1_b.reshape(1, d_model), alpha)

    return _ple_layer(h1_bf, w_ple_gate.astype(BF16), p.astype(BF16), w_ple_up.astype(BF16), h1,
                      ln2_g.reshape(1, d_model), ln2_b.reshape(1, d_model), alpha)


def kernel(x, p, w_in, conv_w, a_log, dt_bias, gdn_norm_w, w_out, ln1_g, ln1_b, w_ple_gate, w_ple_up, ln2_g, ln2_b):
    batch = x.shape[0]
    depth = w_in.shape[0]
    alpha = (2 * depth) ** 0.25
    outs = []
    for b in range(batch):
        h = x[b]
        for i in range(depth):
            h = _layer(h, p[i, b], w_in[i], conv_w[i], a_log[i], dt_bias[i], gdn_norm_w[i], w_out[i],
                       ln1_g[i], ln1_b[i], w_ple_gate[i], w_ple_up[i], ln2_g[i], ln2_b[i], alpha)
        outs.append(h)
    return jnp.stack(outs)
```

```python
import functools

import jax
import jax.numpy as jnp
from jax import lax
from jax.experimental import pallas as pl
from jax.experimental.pallas import tpu as pltpu

HEAD_DIM = 128
GDN_V_HEADS = 16
GDN_K_HEADS = 8
ATT_HEADS = 16
CONV_WIDTH = 5
GDN_CHUNK = 64
DILATIONS = (16, 4, 1)
ATT_RADIUS = 64
ROPE_THETA = 10000.0
LN_EPS = 1e-5
RMS_EPS = 1e-6
NEG_INF = -1e30
Q_SCALE = HEAD_DIM ** -0.5

LANES = 128
SUBLANES = 8
VMEM_LIMIT = 56 * 1024 * 1024

F32 = jnp.float32
BF16 = jnp.bfloat16
HI = lax.Precision.HIGHEST


def _params(sem, vmem=VMEM_LIMIT):
    return pltpu.CompilerParams(dimension_semantics=sem, vmem_limit_bytes=vmem)


def _sigmoid(x):
    return 1.0 / (1.0 + jnp.exp(-x))


def _silu(x):
    return x * _sigmoid(x)


def _dot(a, b):
    return jnp.dot(a.astype(BF16), b.astype(BF16), preferred_element_type=F32)


def _dot_nt(a, b):
    return lax.dot_general(a.astype(BF16), b.astype(BF16), (((1,), (1,)), ((), ())),
                           preferred_element_type=F32)


def _dot_tn(a, b):
    return lax.dot_general(a.astype(BF16), b.astype(BF16), (((0,), (0,)), ((), ())),
                           preferred_element_type=F32)


def _dot_hi(a, b):
    return jnp.dot(a, b, preferred_element_type=F32, precision=HI)


def _epi_none(acc, j):
    return acc


def _epi_silu(acc, j):
    return _silu(acc)


def _epi_rotary(acc, j, cos_ref, sin_ref, *, n_q_blocks):
    cos = cos_ref[...]
    sin = sin_ref[...]
    scale = jnp.where(j < n_q_blocks, Q_SCALE, 1.0).astype(F32)
    outs = []
    for c in range(acc.shape[1] // HEAD_DIM):
        t = acc[:, c * HEAD_DIM:(c + 1) * HEAD_DIM]
        outs.append((t * cos + pltpu.roll(t, HEAD_DIM // 2, axis=1) * sin) * scale)
    return jnp.concatenate(outs, axis=1)


def _epi_gates(acc, j, alog_ref, dtb_ref):
    t = acc + dtb_ref[...]
    softplus = jnp.maximum(t, 0.0) + jnp.log1p(jnp.exp(-jnp.abs(t)))
    g = -jnp.exp(alog_ref[...]) * softplus
    beta = _sigmoid(acc)
    lane = lax.broadcasted_iota(jnp.int32, acc.shape, 1)
    n_gate = 2 * GDN_V_HEADS
    return jnp.where(lane < n_gate, g, jnp.where(lane < 2 * n_gate, beta, 0.0))


def _mm_kernel(x_ref, w_ref, *refs, epilogue):
    o_ref = refs[-1]
    acc = jnp.dot(x_ref[...], w_ref[...], preferred_element_type=F32)
    o_ref[...] = epilogue(acc, pl.program_id(0), *refs[:-1]).astype(o_ref.dtype)


def _project(x, w, name, epilogue=_epi_none, extra=(), extra_specs=(), out_dtype=F32, tm=512, tn=1024):
    m, kd = x.shape
    n = w.shape[1]
    tm, tn = min(tm, m), min(tn, n)
    assert m % tm == 0 and n % tn == 0
    return pl.pallas_call(
        functools.partial(_mm_kernel, epilogue=epilogue),
        grid=(n // tn, m // tm),
        in_specs=[pl.BlockSpec((tm, kd), lambda j, i: (i, 0)),
                  pl.BlockSpec((kd, tn), lambda j, i: (0, j)), *extra_specs],
        out_specs=pl.BlockSpec((tm, tn), lambda j, i: (i, j)),
        out_shape=jax.ShapeDtypeStruct((m, n), out_dtype),
        compiler_params=_params(("parallel", "parallel")),
        name=name,
    )(x, w, *extra)


def _conv_kernel(u_ref, w_ref, o_ref, *, seq, tile, n_q, n_k):
    j = pl.program_id(0)
    n_tiles = seq // tile
    w = w_ref[...]
    pad = (CONV_WIDTH - 1) // 2
    ext_rows = tile + 2 * SUBLANES

    def body(i, carry):
        t0 = pl.multiple_of(i * tile, tile)
        cur = u_ref[pl.ds(t0, tile), :]
        p0 = pl.multiple_of(jnp.maximum(t0 - SUBLANES, 0), SUBLANES)
        n0 = pl.multiple_of(jnp.minimum(t0 + tile, seq - SUBLANES), SUBLANES)
        prev = jnp.where(i > 0, u_ref[pl.ds(p0, SUBLANES), :], 0.0)
        nxt = jnp.where(i < n_tiles - 1, u_ref[pl.ds(n0, SUBLANES), :], 0.0)
        ext = jnp.concatenate([prev, cur, nxt], axis=0)
        acc = jnp.zeros((tile, LANES), F32)
        for tap in range(CONV_WIDTH):
            shift = (pad - tap) % ext_rows
            sh = ext if shift == 0 else pltpu.roll(ext, shift, axis=0)
            acc = acc + sh[SUBLANES:SUBLANES + tile] * w[tap:tap + 1, :]
        y = _silu(acc)
        inv = lax.rsqrt(jnp.sum(y * y, axis=-1, keepdims=True) + RMS_EPS)
        scale = jnp.where(j < n_q, inv * Q_SCALE, jnp.where(j < n_q + n_k, inv, 1.0))
        o_ref[pl.ds(t0, tile), :] = y * scale
        return carry

    lax.fori_loop(0, n_tiles, body, 0)


def _conv_norm(u, conv_w8):
    seq, chans = u.shape
    tile = min(256, seq)
    return pl.pallas_call(
        functools.partial(_conv_kernel, seq=seq, tile=tile, n_q=GDN_K_HEADS, n_k=GDN_K_HEADS),
        grid=(chans // LANES,),
        in_specs=[pl.BlockSpec((seq, LANES), lambda j: (0, j)),
                  pl.BlockSpec((SUBLANES, LANES), lambda j: (0, j))],
        out_specs=pl.BlockSpec((seq, LANES), lambda j: (0, j)),
        out_shape=jax.ShapeDtypeStruct((seq, chans), F32),
        compiler_params=_params(("parallel",)),
        name="gdn_conv_norm",
    )(u, conv_w8)


GDN_SUB = 16


def _unit_tri_inverse_minus_eye(a_list, same_block):
    d = [jnp.where(same_block, a, 0.0) for a in a_list]
    off = [a - x for a, x in zip(a_list, d)]
    p = [-x for x in d]
    e = d
    for _ in range(3):
        e = [_dot(x, x) for x in e]
        p = [x + y + _dot(x, y) for x, y in zip(p, e)]
    m = [y + _dot(x, y) for x, y in zip(p, off)]
    m2 = [_dot(x, x) for x in m]
    q = [y - x - _dot(x, y) for x, y in zip(m, m2)]
    return [x + y + _dot(x, y) for x, y in zip(q, p)]


def _cumsum_dot(tri, g, tri_left):
    hi = g.astype(BF16)
    lo = (g - hi.astype(F32)).astype(BF16)
    t = tri.astype(BF16)
    if tri_left:
        return jnp.dot(t, hi, preferred_element_type=F32) + jnp.dot(t, lo, preferred_element_type=F32)
    return jnp.dot(hi, t, preferred_element_type=F32) + jnp.dot(lo, t, preferred_element_type=F32)


def _gdn_kernel(qf_ref, kf_ref, vf_ref, gcf_ref, grf_ref, qb_ref, kb_ref, vb_ref, gcb_ref, grb_ref,
                of_ref, ob_ref, st_ref, lhs_ref, psi_ref, oin_ref, *, n_chunks, chunk):
    @pl.when(pl.program_id(1) == 0)
    def _():
        st_ref[...] = jnp.zeros_like(st_ref)

    row = lax.broadcasted_iota(jnp.int32, (chunk, chunk), 0)
    col = lax.broadcasted_iota(jnp.int32, (chunk, chunk), 1)
    same_block = (row // GDN_SUB) == (col // GDN_SUB)
    lower_f = (col <= row).astype(F32)
    upper_f = (col >= row).astype(F32)
    dirs = ((qf_ref, kf_ref, vf_ref, gcf_ref, grf_ref, of_ref), (qb_ref, kb_ref, vb_ref, gcb_ref, grb_ref, ob_ref))
    n_e = 2

    a_list, rhs_list, tail_list, gamma_list, qkd_list, qdec_list, k_list = [], [], [], [], [], [], []
    for d, (q_ref, k_ref, v_ref, gc_ref, gr_ref, _) in enumerate(dirs):
        backward = d == 1
        incl = (col >= row) if backward else (col <= row)
        strict = (col > row) if backward else (col < row)
        tri_c = upper_f if backward else lower_f
        tri_r = lower_f if backward else upper_f
        last = 0 if backward else chunk - 1
        for c in range(n_chunks):
            rows = slice(c * chunk, (c + 1) * chunk)
            q = q_ref[rows, :]
            k = k_ref[rows, :]
            gb_col = gc_ref[0, rows, :]
            gb_row = gr_ref[0, c]
            cum_col = _cumsum_dot(tri_c, gb_col, True)
            cum_row = _cumsum_dot(tri_r, gb_row, False)
            kk = _dot_nt(k, k)
            qk = _dot_nt(q, k)
            k_list.append(k)
            for e in range(n_e):
                v = v_ref[rows, e * HEAD_DIM:(e + 1) * HEAD_DIM]
                g_col = cum_col[:, e:e + 1]
                g_row = cum_row[e:e + 1, :]
                beta = gb_col[:, n_e + e:n_e + e + 1]
                g_last = g_col[last:last + 1, :]
                decay = jnp.where(incl, jnp.exp(jnp.where(incl, g_col - g_row, 0.0)), 0.0)
                eg = jnp.exp(g_col)
                a_list.append(jnp.where(strict, kk * decay, 0.0) * beta)
                rhs_list.append(jnp.concatenate([v * beta, k * (beta * eg)], axis=1))
                tail_list.append(jnp.exp(g_last - g_col))
                gamma_list.append(jnp.exp(g_last))
                qkd_list.append(qk * decay)
                qdec_list.append(q * eg)

    t_list = _unit_tri_inverse_minus_eye(a_list, same_block)
    uw_list = [r + _dot(t, r) for t, r in zip(t_list, rhs_list)]
    qo_list = [_dot(x, y) for x, y in zip(qkd_list, uw_list)]
    for dc in range(2 * n_chunks):
        scaled = jnp.concatenate([uw_list[n_e * dc + e] * tail_list[n_e * dc + e] for e in range(n_e)], axis=1)
        kt = _dot_tn(k_list[dc], scaled)
        for e in range(n_e):
            b = n_e * dc + e
            base = 2 * HEAD_DIM * e
            psi_ref[b] = kt[:, base:base + HEAD_DIM]
            lhs_ref[b, :HEAD_DIM, :] = (-kt[:, base + HEAD_DIM:base + 2 * HEAD_DIM]).astype(BF16)
            lhs_ref[b, HEAD_DIM:, :] = (qdec_list[b] - qo_list[b][:, HEAD_DIM:]).astype(BF16)
            oin_ref[b] = qo_list[b][:, :HEAD_DIM]

    states = [st_ref[i] for i in range(2 * n_e)]
    for step in range(n_chunks):
        for d in range(2):
            c = n_chunks - 1 - step if d == 1 else step
            o_ref = dirs[d][5]
            for e in range(n_e):
                b = n_e * (d * n_chunks + c) + e
                s = states[n_e * d + e]
                res = jnp.dot(lhs_ref[b], s.astype(BF16), preferred_element_type=F32)
                states[n_e * d + e] = gamma_list[b] * s + res[:HEAD_DIM] + psi_ref[b]
                o_ref[c * chunk:(c + 1) * chunk, e * HEAD_DIM:(e + 1) * HEAD_DIM] = res[HEAD_DIM:] + oin_ref[b]
    for i in range(2 * n_e):
        st_ref[i] = states[i]


def _gated_delta(qkv, g_cols, g_rows, n_chunks=4):
    seq = qkv.shape[0]
    blk = n_chunks * GDN_CHUNK
    nb = seq // blk
    rep = GDN_V_HEADS // GDN_K_HEADS
    vw = rep * HEAD_DIM
    v_off = 2 * GDN_K_HEADS * HEAD_DIM // vw
    n_chains = 2 * n_chunks * rep

    def specs(rowmap):
        return [pl.BlockSpec((blk, HEAD_DIM), lambda j, n: (rowmap(n), j)),
                pl.BlockSpec((blk, HEAD_DIM), lambda j, n: (rowmap(n), GDN_K_HEADS + j)),
                pl.BlockSpec((blk, vw), lambda j, n: (rowmap(n), v_off + j)),
                pl.BlockSpec((1, blk, SUBLANES), lambda j, n: (j, rowmap(n), 0)),
                pl.BlockSpec((1, n_chunks, SUBLANES, GDN_CHUNK), lambda j, n: (j, rowmap(n), 0, 0))]

    fwd = lambda n: n
    bwd = lambda n: nb - 1 - n
    out_sd = jax.ShapeDtypeStruct((seq, GDN_V_HEADS * HEAD_DIM), F32)
    return pl.pallas_call(
        functools.partial(_gdn_kernel, n_chunks=n_chunks, chunk=GDN_CHUNK),
        grid=(GDN_K_HEADS, nb),
        in_specs=specs(fwd) + specs(bwd),
        out_specs=[pl.BlockSpec((blk, vw), lambda j, n: (fwd(n), j)),
                   pl.BlockSpec((blk, vw), lambda j, n: (bwd(n), j))],
        out_shape=[out_sd, out_sd],
        scratch_shapes=[pltpu.VMEM((2 * rep, HEAD_DIM, HEAD_DIM), F32),
                        pltpu.VMEM((n_chains, HEAD_DIM + GDN_CHUNK, HEAD_DIM), BF16),
                        pltpu.VMEM((n_chains, HEAD_DIM, HEAD_DIM), F32),
                        pltpu.VMEM((n_chains, GDN_CHUNK, HEAD_DIM), F32)],
        compiler_params=_params(("parallel", "arbitrary")),
        name="gdn_delta_rule",
    )(qkv, qkv, qkv, g_cols[0], g_rows[0], qkv, qkv, qkv, g_cols[1], g_rows[1])


ATT_TQ = 256
ATT_OFFSETS = 3


def _att_kernel(q_ref, k_ref, v_ref, z_ref, o_ref, acc_ref, ml_ref, band_ref, *, seq):
    lane = lax.broadcasted_iota(jnp.int32, (1, LANES), 1)
    rel = (lax.broadcasted_iota(jnp.int32, band_ref.shape[1:], 0)
           - lax.broadcasted_iota(jnp.int32, band_ref.shape[1:], 1))
    for i in range(ATT_OFFSETS):
        band_ref[i] = (jnp.abs(rel + i * ATT_RADIUS) <= ATT_RADIUS).astype(F32)

    for p, dil in enumerate(DILATIONS):
        length = seq // dil
        tq = min(ATT_TQ, length)
        win = min(tq + 2 * ATT_RADIUS, length)
        tiles = length // tq
        first, final = p == 0, p == len(DILATIONS) - 1
        n_it = dil * tiles
        per_trip = 2 if n_it % 2 == 0 else 1

        def tile(it, dil=dil, tq=tq, win=win, tiles=tiles, length=length, first=first, final=final):
            r = it // tiles
            m0 = (it % tiles) * tq
            k0 = jnp.clip(m0 - ATT_RADIUS, 0, length - win)
            if dil == 1:
                qsl = pl.ds(pl.multiple_of(m0, tq), tq)
                ksl = pl.ds(pl.multiple_of(k0, ATT_RADIUS), win)
            else:
                qsl = pl.ds(r + dil * m0, tq, stride=dil)
                ksl = pl.ds(r + dil * k0, win, stride=dil)
            s = _dot_nt(q_ref[qsl, :], k_ref[ksl, :])
            if win == tq + 2 * ATT_RADIUS:
                valid = band_ref[(m0 - k0) // ATT_RADIUS, :tq, :win] > 0.0
            else:
                qpos = m0 + lax.broadcasted_iota(jnp.int32, (tq, win), 0)
                kpos = k0 + lax.broadcasted_iota(jnp.int32, (tq, win), 1)
                valid = jnp.abs(qpos - kpos) <= ATT_RADIUS
            s = jnp.where(valid, s, NEG_INF)
            m_new = jnp.max(s, axis=-1, keepdims=True)
            if not first:
                ml = ml_ref[qsl, :]
                m_old, l_old = ml[:, 0:1], ml[:, LANES // 2:LANES // 2 + 1]
                m_new = jnp.maximum(m_old, m_new)
            e = jnp.exp(s - m_new)
            l_new = jnp.sum(e, axis=-1, keepdims=True)
            acc = _dot(e, v_ref[ksl, :])
            if not first:
                corr = jnp.exp(m_old - m_new)
                l_new = l_new + corr * l_old
                acc = acc + corr * acc_ref[qsl, :]
            if final:
                return qsl, (acc / l_new * _silu(z_ref[qsl, :])).astype(o_ref.dtype), None
            return qsl, acc, jnp.where(lane < LANES // 2, m_new, l_new)

        def body(trip, carry, per_trip=per_trip, final=final, tile=tile):
            done = [tile(trip * per_trip + t) for t in range(per_trip)]
            for qsl, main, ml in done:
                if final:
                    o_ref[qsl, :] = main
                else:
                    acc_ref[qsl, :] = main
                    ml_ref[qsl, :] = ml
            return carry

        lax.fori_loop(0, n_it // per_trip, body, 0)


def _dilated_attention(qk, v, z):
    seq, width = v.shape
    heads = width // HEAD_DIM
    col = pl.BlockSpec((seq, HEAD_DIM), lambda h: (0, h))
    return pl.pallas_call(
        functools.partial(_att_kernel, seq=seq),
        grid=(heads,),
        in_specs=[col, pl.BlockSpec((seq, HEAD_DIM), lambda h: (0, heads + h)), col, col],
        out_specs=col,
        out_shape=jax.ShapeDtypeStruct((seq, width), BF16),
        scratch_shapes=[pltpu.VMEM((seq, HEAD_DIM), F32), pltpu.VMEM((seq, LANES), F32),
                        pltpu.VMEM((ATT_OFFSETS, ATT_TQ, ATT_TQ + 2 * ATT_RADIUS), F32)],
        compiler_params=_params(("parallel",)),
        name="dilated_attention",
    )(qk, qk, v, z)


def _gdn_out_kernel(of_ref, ob_ref, z_ref, w_ref, y_ref):
    w = w_ref[...]
    for h in range(of_ref.shape[1] // HEAD_DIM):
        sl = slice(h * HEAD_DIM, (h + 1) * HEAD_DIM)
        o = of_ref[:, sl] + ob_ref[:, sl]
        inv = lax.rsqrt(jnp.mean(o * o, axis=-1, keepdims=True) + RMS_EPS)
        y_ref[:, sl] = (o * inv * w * z_ref[:, sl]).astype(y_ref.dtype)


def _gdn_out(o_f, o_b, z_silu, norm_w):
    seq, width = o_f.shape
    tm = min(512, seq)
    blk = pl.BlockSpec((tm, width), lambda i: (i, 0))
    return pl.pallas_call(
        _gdn_out_kernel,
        grid=(seq // tm,),
        in_specs=[blk, blk, blk, pl.BlockSpec((1, HEAD_DIM), lambda i: (0, 0))],
        out_specs=blk,
        out_shape=jax.ShapeDtypeStruct((seq, width), BF16),
        compiler_params=_params(("parallel",)),
        name="gdn_norm_gate",
    )(o_f, o_b, z_silu, norm_w)


def _layer_norm_rows(pre_ref, g_ref, b_ref, out_refs, n_col, tn):
    width = n_col * tn
    total = jnp.zeros((pre_ref.shape[1], 1), F32)
    for c in range(n_col):
        total = total + jnp.sum(pre_ref[c], axis=-1, keepdims=True)
    mu = total / width
    sq = jnp.zeros_like(total)
    for c in range(n_col):
        dev = pre_ref[c] - mu
        sq = sq + jnp.sum(dev * dev, axis=-1, keepdims=True)
    inv = lax.rsqrt(sq / width + LN_EPS)
    for c in range(n_col):
        sl = slice(c * tn, (c + 1) * tn)
        y = (pre_ref[c] - mu) * inv * g_ref[:, sl] + b_ref[:, sl]
        for ref in out_refs:
            ref[:, sl] = y.astype(ref.dtype)


def _outproj_kernel(ya_ref, yb_ref, wa_ref, wb_ref, x_ref, g_ref, b_ref, h_ref, pre_ref, *, n_col, tn, alpha):
    n = pl.program_id(1)
    mix = (jnp.dot(ya_ref[...], wa_ref[...], preferred_element_type=F32)
           + jnp.dot(yb_ref[...], wb_ref[...], preferred_element_type=F32))
    pre_ref[n] = alpha * x_ref[...] + mix

    @pl.when(n == n_col - 1)
    def _():
        _layer_norm_rows(pre_ref, g_ref, b_ref, (h_ref,), n_col, tn)


def _out_projection(y_a, y_b, w_a, w_b, x, gamma, beta, alpha, tm=512, tn=512):
    seq, d_model = x.shape
    tm = min(tm, seq)
    n_col = d_model // tn
    ka, kb = y_a.shape[1], y_b.shape[1]
    row = lambda i, n: (i, 0)
    full = pl.BlockSpec((tm, d_model), row)
    vec = pl.BlockSpec((1, d_model), lambda i, n: (0, 0))
    return pl.pallas_call(
        functools.partial(_outproj_kernel, n_col=n_col, tn=tn, alpha=alpha),
        grid=(seq // tm, n_col),
        in_specs=[pl.BlockSpec((tm, ka), row), pl.BlockSpec((tm, kb), row),
                  pl.BlockSpec((ka, tn), lambda i, n: (0, n)), pl.BlockSpec((kb, tn), lambda i, n: (0, n)),
                  pl.BlockSpec((tm, tn), lambda i, n: (i, n)), vec, vec],
        out_specs=full,
        out_shape=jax.ShapeDtypeStruct((seq, d_model), F32),
        scratch_shapes=[pltpu.VMEM((n_col, tm, tn), F32)],
        compiler_params=_params(("parallel", "arbitrary")),
        name="out_proj_ln",
    )(y_a, y_b, w_a, w_b, x, gamma, beta)


def _ple_kernel(hb_ref, wg_ref, p_ref, wu_ref, h_ref, g_ref, b_ref, o_ref, pre_ref, *, n_col, tn, alpha):
    n = pl.program_id(1)
    gate = _sigmoid(jnp.dot(hb_ref[...], wg_ref[...], preferred_element_type=F32))
    up = jnp.dot(p_ref[...], wu_ref[...], preferred_element_type=F32)
    pre_ref[n] = alpha * h_ref[...] + gate * up

    @pl.when(n == n_col - 1)
    def _():
        _layer_norm_rows(pre_ref, g_ref, b_ref, (o_ref,), n_col, tn)


def _ple_layer(h_bf, w_gate, p_bf, w_up, h, gamma, beta, alpha, tm=512, tn=512):
    seq, d_model = h.shape
    tm = min(tm, seq)
    n_col = d_model // tn
    pd = p_bf.shape[1]
    row = lambda i, n: (i, 0)
    vec = pl.BlockSpec((1, d_model), lambda i, n: (0, 0))
    return pl.pallas_call(
        functools.partial(_ple_kernel, n_col=n_col, tn=tn, alpha=alpha),
        grid=(seq // tm, n_col),
        in_specs=[pl.BlockSpec((tm, d_model), row), pl.BlockSpec((d_model, tn), lambda i, n: (0, n)),
                  pl.BlockSpec((tm, pd), row), pl.BlockSpec((pd, tn), lambda i, n: (0, n)),
                  pl.BlockSpec((tm, tn), lambda i, n: (i, n)), vec, vec],
        out_specs=pl.BlockSpec((tm, d_model), row),
        out_shape=jax.ShapeDtypeStruct((seq, d_model), F32),
        scratch_shapes=[pltpu.VMEM((n_col, tm, tn), F32)],
        compiler_params=_params(("parallel", "arbitrary")),
        name="ple_ln",
    )(h_bf, w_gate, p_bf, w_up, h, gamma, beta)


def _rotary_tables(seq):
    half = HEAD_DIM // 2
    inv_freq = 1.0 / (jnp.float32(ROPE_THETA) ** (jnp.arange(half, dtype=F32) / half))
    ang = jnp.arange(seq).astype(F32)[:, None] * inv_freq[None, :]
    cos, sin = jnp.cos(ang), jnp.sin(ang)
    return jnp.concatenate([cos, cos], axis=-1), jnp.concatenate([-sin, sin], axis=-1)


def _layer(h, p, w_in, conv_w, a_log, dt_bias, gdn_norm_w, w_out, ln1_g, ln1_b, w_ple_gate, w_ple_up,
           ln2_g, ln2_b, alpha):
    seq, d_model = h.shape
    key_dim = GDN_K_HEADS * HEAD_DIM
    d_gdn = GDN_V_HEADS * HEAD_DIM
    d_att = ATT_HEADS * HEAD_DIM
    n_gate = 2 * GDN_V_HEADS
    o_qkv = 0
    o_za = 2 * key_dim + d_gdn
    o_ab = o_za + d_gdn
    o_qk = o_ab + 2 * n_gate
    o_vb = o_qk + 2 * d_att
    o_zb = o_vb + d_att

    x_bf = h.astype(BF16)
    w = lambda lo, hi: w_in[:, lo:hi].astype(BF16)

    qkv_a = _project(x_bf, w(o_qkv, o_za), "proj_gdn_qkv")
    za = _project(x_bf, w(o_za, o_ab), "proj_gdn_gate", _epi_silu)
    w_ab = jnp.pad(w(o_ab, o_qk), ((0, 0), (0, LANES - 2 * n_gate)))
    pad_vec = lambda t: jnp.pad(t.reshape(1, n_gate).astype(F32), ((0, 0), (0, LANES - n_gate)))
    vec_spec = pl.BlockSpec((1, LANES), lambda j, i: (0, 0))
    gates = _project(x_bf, w_ab, "proj_gdn_decay", _epi_gates, extra=(pad_vec(a_log), pad_vec(dt_bias)),
                     extra_specs=(vec_spec, vec_spec))
    cos, sin = _rotary_tables(seq)
    tm_rot = min(512, seq)
    tab_spec = pl.BlockSpec((tm_rot, HEAD_DIM), lambda j, i: (i, 0))
    tn_rot = 1024
    qk_b = _project(x_bf, w(o_qk, o_vb), "proj_att_qk",
                    functools.partial(_epi_rotary, n_q_blocks=d_att // tn_rot),
                    extra=(cos, sin), extra_specs=(tab_spec, tab_spec), tm=tm_rot, tn=tn_rot)
    v_b = _project(x_bf, w(o_vb, o_zb), "proj_att_v")
    z_b = _project(x_bf, w(o_zb, o_zb + d_att), "proj_att_gate")

    conv_w8 = jnp.pad(conv_w.astype(F32), ((0, SUBLANES - CONV_WIDTH), (0, 0)))
    qkv_n = _conv_norm(qkv_a, conv_w8)

    rep = GDN_V_HEADS // GDN_K_HEADS
    gg = gates[:, :n_gate].reshape(seq, 2, GDN_K_HEADS, rep)
    bb = gates[:, n_gate:2 * n_gate].reshape(seq, 2, GDN_K_HEADS, rep)
    gb = jnp.concatenate([gg, bb, jnp.zeros((seq, 2, GDN_K_HEADS, SUBLANES - 2 * rep), F32)], axis=-1)
    g_cols = gb.transpose(1, 2, 0, 3)
    g_rows = gb.reshape(seq // GDN_CHUNK, GDN_CHUNK, 2, GDN_K_HEADS, SUBLANES).transpose(2, 3, 0, 4, 1)
    o_f, o_b = _gated_delta(qkv_n, g_cols, g_rows)

    y_gdn = _gdn_out(o_f, o_b, za, gdn_norm_w.reshape(1, HEAD_DIM).astype(F32))

    y_att = _dilated_attention(qk_b, v_b, z_b)

    w_o = w_out.astype(BF16)
    h1 = _out_projection(y_gdn, y_att, w_o[:d_gdn], w_o[d_gdn:], h,
                         ln1_g.reshape(1, d_model), ln1_b.reshape(1, d_model), alpha)

    return _ple_layer(h1.astype(BF16), w_ple_gate.astype(BF16), p.astype(BF16), w_ple_up.astype(BF16), h1,
                      ln2_g.reshape(1, d_model), ln2_b.reshape(1, d_model), alpha)


def kernel(x, p, w_in, conv_w, a_log, dt_bias, gdn_norm_w, w_out, ln1_g, ln1_b, w_ple_gate, w_ple_up, ln2_g, ln2_b):
    batch = x.shape[0]
    depth = w_in.shape[0]
    alpha = (2 * depth) ** 0.25
    outs = []
    for b in range(batch):
        h = x[b]
        for i in range(depth):
            h = _layer(h, p[i, b], w_in[i], conv_w[i], a_log[i], dt_bias[i], gdn_norm_w[i], w_out[i],
                       ln1_g[i], ln1_b[i], w_ple_gate[i], w_ple_up[i], ln2_g[i], ln2_b[i], alpha)
        outs.append(h)
    return jnp.stack(outs)
```

```python
import functools

import jax
import jax.numpy as jnp
from jax import lax
from jax.experimental import pallas as pl
from jax.experimental.pallas import tpu as pltpu

HEAD_DIM = 128
GDN_V_HEADS = 16
GDN_K_HEADS = 8
ATT_HEADS = 16
CONV_WIDTH = 5
GDN_CHUNK = 64
DILATIONS = (16, 4, 1)
ATT_RADIUS = 64
ROPE_THETA = 10000.0
LN_EPS = 1e-5
RMS_EPS = 1e-6
NEG_INF = -1e30
Q_SCALE = HEAD_DIM ** -0.5

LANES = 128
SUBLANES = 8
VMEM_LIMIT = 56 * 1024 * 1024

F32 = jnp.float32
BF16 = jnp.bfloat16
HI = lax.Precision.HIGHEST


def _params(sem, vmem=VMEM_LIMIT):
    return pltpu.CompilerParams(dimension_semantics=sem, vmem_limit_bytes=vmem)


def _sigmoid(x):
    return 1.0 / (1.0 + jnp.exp(-x))


def _silu(x):
    return x * _sigmoid(x)


def _dot(a, b):
    return jnp.dot(a.astype(BF16), b.astype(BF16), preferred_element_type=F32)


def _dot_nt(a, b):
    return lax.dot_general(a.astype(BF16), b.astype(BF16), (((1,), (1,)), ((), ())),
                           preferred_element_type=F32)


def _dot_tn(a, b):
    return lax.dot_general(a.astype(BF16), b.astype(BF16), (((0,), (0,)), ((), ())),
                           preferred_element_type=F32)


def _dot_hi(a, b):
    return jnp.dot(a, b, preferred_element_type=F32, precision=HI)


def _epi_none(acc, j):
    return acc


def _epi_silu(acc, j):
    return _silu(acc)


def _epi_rotary(acc, j, cos_ref, sin_ref, *, n_q_blocks):
    cos = cos_ref[...]
    sin = sin_ref[...]
    scale = jnp.where(j < n_q_blocks, Q_SCALE, 1.0).astype(F32)
    outs = []
    for c in range(acc.shape[1] // HEAD_DIM):
        t = acc[:, c * HEAD_DIM:(c + 1) * HEAD_DIM]
        outs.append((t * cos + pltpu.roll(t, HEAD_DIM // 2, axis=1) * sin) * scale)
    return jnp.concatenate(outs, axis=1)


def _epi_gates(acc, j, alog_ref, dtb_ref):
    t = acc + dtb_ref[...]
    softplus = jnp.maximum(t, 0.0) + jnp.log1p(jnp.exp(-jnp.abs(t)))
    g = -jnp.exp(alog_ref[...]) * softplus
    beta = _sigmoid(acc)
    lane = lax.broadcasted_iota(jnp.int32, acc.shape, 1)
    n_gate = 2 * GDN_V_HEADS
    return jnp.where(lane < n_gate, g, jnp.where(lane < 2 * n_gate, beta, 0.0))


def _mm_kernel(x_ref, w_ref, *refs, epilogue):
    o_ref = refs[-1]
    acc = jnp.dot(x_ref[...], w_ref[...], preferred_element_type=F32)
    o_ref[...] = epilogue(acc, pl.program_id(0), *refs[:-1]).astype(o_ref.dtype)


def _project(x, w, name, epilogue=_epi_none, extra=(), extra_specs=(), out_dtype=F32, tm=512, tn=1024):
    m, kd = x.shape
    n = w.shape[1]
    tm, tn = min(tm, m), min(tn, n)
    assert m % tm == 0 and n % tn == 0
    return pl.pallas_call(
        functools.partial(_mm_kernel, epilogue=epilogue),
        grid=(n // tn, m // tm),
        in_specs=[pl.BlockSpec((tm, kd), lambda j, i: (i, 0)),
                  pl.BlockSpec((kd, tn), lambda j, i: (0, j)), *extra_specs],
        out_specs=pl.BlockSpec((tm, tn), lambda j, i: (i, j)),
        out_shape=jax.ShapeDtypeStruct((m, n), out_dtype),
        compiler_params=_params(("parallel", "parallel")),
        name=name,
    )(x, w, *extra)


def _conv_kernel(u_ref, w_ref, o_ref, *, seq, tile, n_q, n_k):
    j = pl.program_id(0)
    n_tiles = seq // tile
    w = w_ref[...]
    pad = (CONV_WIDTH - 1) // 2
    ext_rows = tile + 2 * SUBLANES

    def body(i, carry):
        t0 = pl.multiple_of(i * tile, tile)
        cur = u_ref[pl.ds(t0, tile), :]
        p0 = pl.multiple_of(jnp.maximum(t0 - SUBLANES, 0), SUBLANES)
        n0 = pl.multiple_of(jnp.minimum(t0 + tile, seq - SUBLANES), SUBLANES)
        prev = jnp.where(i > 0, u_ref[pl.ds(p0, SUBLANES), :], 0.0)
        nxt = jnp.where(i < n_tiles - 1, u_ref[pl.ds(n0, SUBLANES), :], 0.0)
        ext = jnp.concatenate([prev, cur, nxt], axis=0)
        acc = jnp.zeros((tile, LANES), F32)
        for tap in range(CONV_WIDTH):
            shift = (pad - tap) % ext_rows
            sh = ext if shift == 0 else pltpu.roll(ext, shift, axis=0)
            acc = acc + sh[SUBLANES:SUBLANES + tile] * w[tap:tap + 1, :]
        y = _silu(acc)
        inv = lax.rsqrt(jnp.sum(y * y, axis=-1, keepdims=True) + RMS_EPS)
        scale = jnp.where(j < n_q, inv * Q_SCALE, jnp.where(j < n_q + n_k, inv, 1.0))
        o_ref[pl.ds(t0, tile), :] = y * scale
        return carry

    lax.fori_loop(0, n_tiles, body, 0)


def _conv_norm(u, conv_w8):
    seq, chans = u.shape
    tile = min(256, seq)
    return pl.pallas_call(
        functools.partial(_conv_kernel, seq=seq, tile=tile, n_q=GDN_K_HEADS, n_k=GDN_K_HEADS),
        grid=(chans // LANES,),
        in_specs=[pl.BlockSpec((seq, LANES), lambda j: (0, j)),
                  pl.BlockSpec((SUBLANES, LANES), lambda j: (0, j))],
        out_specs=pl.BlockSpec((seq, LANES), lambda j: (0, j)),
        out_shape=jax.ShapeDtypeStruct((seq, chans), F32),
        compiler_params=_params(("parallel",)),
        name="gdn_conv_norm",
    )(u, conv_w8)


GDN_SUB = 16


def _pair_block_diag(x, left):
    xb = x.astype(BF16)
    zero = jnp.zeros_like(xb)
    return jnp.concatenate([jnp.where(left, xb, zero), jnp.where(left, zero, xb)], axis=0)


def _pair_dot(x, y, left):
    return jnp.dot(x.astype(BF16), _pair_block_diag(y, left), preferred_element_type=F32)


def _unit_tri_inverse_minus_eye(a_list, same_block, left):
    dot = functools.partial(_pair_dot, left=left)
    d = [jnp.where(same_block, a, 0.0) for a in a_list]
    off = [a - x for a, x in zip(a_list, d)]
    p = [-x for x in d]
    e = d
    for _ in range(3):
        e = [dot(x, x) for x in e]
        p = [x + y + dot(x, y) for x, y in zip(p, e)]
    m = [y + dot(x, y) for x, y in zip(p, off)]
    m2 = [dot(x, x) for x in m]
    q = [y - x - dot(x, y) for x, y in zip(m, m2)]
    return [x + y + dot(x, y) for x, y in zip(q, p)]


def _dot_split(a, b_bf):
    hi = a.astype(BF16)
    lo = (a - hi.astype(F32)).astype(BF16)
    return jnp.dot(hi, b_bf, preferred_element_type=F32) + jnp.dot(lo, b_bf, preferred_element_type=F32)


def _split_dot(a_bf, b):
    hi = b.astype(BF16)
    lo = (b - hi.astype(F32)).astype(BF16)
    return jnp.dot(a_bf, hi, preferred_element_type=F32) + jnp.dot(a_bf, lo, preferred_element_type=F32)


def _gdn_kernel(qf_ref, kf_ref, vf_ref, gf_ref, qb_ref, kb_ref, vb_ref, gb_ref,
                of_ref, ob_ref, st_ref, lhs_ref, psi_ref, oin_ref, *, n_chunks, chunk):
    j = pl.program_id(0)

    @pl.when(pl.program_id(1) == 0)
    def _():
        st_ref[...] = jnp.zeros_like(st_ref)

    n_e = 2
    width = n_e * chunk
    row = lax.broadcasted_iota(jnp.int32, (chunk, width), 0)
    lane = lax.broadcasted_iota(jnp.int32, (chunk, width), 1)
    col = lane % chunk
    left = lane < chunk
    same_block = (row // GDN_SUB) == (col // GDN_SUB)
    r2 = lax.broadcasted_iota(jnp.int32, (chunk, chunk), 0)
    c2 = lax.broadcasted_iota(jnp.int32, (chunk, chunk), 1)
    sel_row = lax.broadcasted_iota(jnp.int32, (LANES, LANES), 0)
    sel_lane = lax.broadcasted_iota(jnp.int32, (LANES, LANES), 1)
    dirs = ((qf_ref, kf_ref, vf_ref, gf_ref, of_ref), (qb_ref, kb_ref, vb_ref, gb_ref, ob_ref))

    a_list, qkd_list, rhs_list, tail_list, k_list, gamma_list, qdec_list = [], [], [], [], [], [], []
    for d, (q_ref, k_ref, v_ref, g_ref, _) in enumerate(dirs):
        backward = d == 1
        incl = (col >= row) if backward else (col <= row)
        strict = (col > row) if backward else (col < row)
        tri_c = ((c2 >= r2) if backward else (c2 <= r2)).astype(BF16)
        tri_r = ((row >= col) if backward else (row <= col)).astype(F32)
        last = 0 if backward else chunk - 1
        src = d * GDN_V_HEADS + n_e * j + jnp.where(sel_lane < n_e, sel_lane, 2 * GDN_V_HEADS + sel_lane - n_e)
        sel = ((sel_row == src) & (sel_lane < 2 * n_e)).astype(BF16)
        picked_all = _dot_split(g_ref[...], sel)
        for c in range(n_chunks):
            rows = slice(c * chunk, (c + 1) * chunk)
            q = q_ref[rows, :]
            k = k_ref[rows, :]
            picked = picked_all[rows]
            cum = _split_dot(tri_c, picked)
            rep = lambda t, i: jnp.broadcast_to(t[:, i:i + 1], (chunk, LANES))
            g_rep = [rep(picked, e) for e in range(n_e)]
            b_rep = [rep(picked, n_e + e) for e in range(n_e)]
            cum_rep = [rep(cum, e) for e in range(n_e)]
            g_pair = jnp.where(left, g_rep[0], g_rep[1])
            cum_col = jnp.where(left, cum_rep[0], cum_rep[1])
            cum_row = jnp.sum(g_pair * tri_r, axis=0, keepdims=True)
            decay = jnp.where(incl, jnp.exp(jnp.where(incl, cum_col - cum_row, 0.0)), 0.0)
            k2 = jnp.concatenate([k, k], axis=0)
            kk = _dot_nt(k, k2)
            qk = _dot_nt(q, k2)
            a_list.append(jnp.where(strict, kk * decay, 0.0) * jnp.where(left, b_rep[0], b_rep[1]))
            qkd_list.append(qk * decay)
            k_list.append(k)
            rhs, tail = [], []
            for e in range(n_e):
                v = v_ref[rows, e * HEAD_DIM:(e + 1) * HEAD_DIM]
                g_last = cum_rep[e][last:last + 1, :]
                eg = jnp.exp(cum_rep[e])
                rhs.append(jnp.concatenate([v * b_rep[e], k * (b_rep[e] * eg)], axis=1))
                tail.append(jnp.exp(g_last - cum_rep[e]))
                gamma_list.append(jnp.exp(g_last))
                qdec_list.append(q * eg)
            rhs_list.append(jnp.concatenate(rhs, axis=0))
            tail_list.append(jnp.concatenate(tail, axis=0))

    t_list = _unit_tri_inverse_minus_eye(a_list, same_block, left)
    uw_list = [r + jnp.dot(_pair_block_diag(t, left), r.astype(BF16), preferred_element_type=F32)
               for t, r in zip(t_list, rhs_list)]
    qo_list = [jnp.dot(_pair_block_diag(x, left), y.astype(BF16), preferred_element_type=F32)
               for x, y in zip(qkd_list, uw_list)]
    for dc in range(2 * n_chunks):
        uw, tail = uw_list[dc], tail_list[dc]
        scaled = jnp.concatenate([uw[e * chunk:(e + 1) * chunk, h * HEAD_DIM:(h + 1) * HEAD_DIM]
                                  * tail[e * chunk:(e + 1) * chunk] for e in range(n_e) for h in range(2)],
                                 axis=1)
        kt = _dot_tn(k_list[dc], scaled)
        for e in range(n_e):
            b = n_e * dc + e
            base = 2 * HEAD_DIM * e
            qo = qo_list[dc][e * chunk:(e + 1) * chunk]
            psi_ref[b] = kt[:, base:base + HEAD_DIM]
            lhs_ref[b, :HEAD_DIM, :] = (-kt[:, base + HEAD_DIM:base + 2 * HEAD_DIM]).astype(BF16)
            lhs_ref[b, HEAD_DIM:, :] = (qdec_list[b] - qo[:, HEAD_DIM:]).astype(BF16)
            oin_ref[b] = qo[:, :HEAD_DIM]

    states = [st_ref[i] for i in range(2 * n_e)]
    for step in range(n_chunks):
        for d in range(2):
            c = n_chunks - 1 - step if d == 1 else step
            o_ref = dirs[d][4]
            for e in range(n_e):
                b = n_e * (d * n_chunks + c) + e
                s = states[n_e * d + e]
                res = jnp.dot(lhs_ref[b], s.astype(BF16), preferred_element_type=F32)
                states[n_e * d + e] = gamma_list[b] * s + res[:HEAD_DIM] + psi_ref[b]
                o_ref[c * chunk:(c + 1) * chunk, e * HEAD_DIM:(e + 1) * HEAD_DIM] = res[HEAD_DIM:] + oin_ref[b]
    for i in range(2 * n_e):
        st_ref[i] = states[i]


def _gated_delta(qkv, gates, n_chunks=8):
    seq = qkv.shape[0]
    blk = n_chunks * GDN_CHUNK
    nb = seq // blk
    rep = GDN_V_HEADS // GDN_K_HEADS
    vw = rep * HEAD_DIM
    v_off = 2 * GDN_K_HEADS * HEAD_DIM // vw
    n_chains = 2 * n_chunks * rep

    def specs(rowmap):
        return [pl.BlockSpec((blk, HEAD_DIM), lambda j, n: (rowmap(n), j)),
                pl.BlockSpec((blk, HEAD_DIM), lambda j, n: (rowmap(n), GDN_K_HEADS + j)),
                pl.BlockSpec((blk, vw), lambda j, n: (rowmap(n), v_off + j)),
                pl.BlockSpec((blk, LANES), lambda j, n: (rowmap(n), 0))]

    fwd = lambda n: n
    bwd = lambda n: nb - 1 - n
    out_sd = jax.ShapeDtypeStruct((seq, GDN_V_HEADS * HEAD_DIM), F32)
    return pl.pallas_call(
        functools.partial(_gdn_kernel, n_chunks=n_chunks, chunk=GDN_CHUNK),
        grid=(GDN_K_HEADS, nb),
        in_specs=specs(fwd) + specs(bwd),
        out_specs=[pl.BlockSpec((blk, vw), lambda j, n: (fwd(n), j)),
                   pl.BlockSpec((blk, vw), lambda j, n: (bwd(n), j))],
        out_shape=[out_sd, out_sd],
        scratch_shapes=[pltpu.VMEM((2 * rep, HEAD_DIM, HEAD_DIM), F32),
                        pltpu.VMEM((n_chains, HEAD_DIM + GDN_CHUNK, HEAD_DIM), BF16),
                        pltpu.VMEM((n_chains, HEAD_DIM, HEAD_DIM), F32),
                        pltpu.VMEM((n_chains, GDN_CHUNK, HEAD_DIM), F32)],
        compiler_params=_params(("parallel", "arbitrary")),
        name="gdn_delta_rule",
    )(qkv, qkv, qkv, gates, qkv, qkv, qkv, gates)


ATT_TQ = 256
ATT_OFFSETS = 3


def _att_kernel(q_ref, k_ref, v_ref, z_ref, o_ref, acc_ref, m_ref, l_ref, band_ref, *, seq):
    rel = (lax.broadcasted_iota(jnp.int32, band_ref.shape[1:], 0)
           - lax.broadcasted_iota(jnp.int32, band_ref.shape[1:], 1))
    for i in range(ATT_OFFSETS):
        band_ref[i] = (jnp.abs(rel + i * ATT_RADIUS) <= ATT_RADIUS).astype(F32)

    for p, dil in enumerate(DILATIONS):
        length = seq // dil
        tq = min(ATT_TQ, length)
        win = min(tq + 2 * ATT_RADIUS, length)
        tiles = length // tq
        first, final = p == 0, p == len(DILATIONS) - 1
        n_it = dil * tiles
        per_trip = max(t for t in (4, 2, 1) if n_it % t == 0)

        def scores(it, dil=dil, tq=tq, win=win, tiles=tiles, length=length):
            r = it // tiles
            m0 = (it % tiles) * tq
            k0 = jnp.clip(m0 - ATT_RADIUS, 0, length - win)
            if dil == 1:
                qsl = pl.ds(pl.multiple_of(m0, tq), tq)
                ksl = pl.ds(pl.multiple_of(k0, ATT_RADIUS), win)
            else:
                qsl = pl.ds(r + dil * m0, tq, stride=dil)
                ksl = pl.ds(r + dil * k0, win, stride=dil)
            s = _dot_nt(q_ref[qsl, :], k_ref[ksl, :])
            if win == tq + 2 * ATT_RADIUS:
                valid = band_ref[(m0 - k0) // ATT_RADIUS, :tq, :win] > 0.0
            else:
                qpos = m0 + lax.broadcasted_iota(jnp.int32, (tq, win), 0)
                kpos = k0 + lax.broadcasted_iota(jnp.int32, (tq, win), 1)
                valid = jnp.abs(qpos - kpos) <= ATT_RADIUS
            return qsl, ksl, jnp.where(valid, s, NEG_INF)

        def tile(qsl, ksl, s, tq=tq, win=win, first=first, final=final):
            slabs = [s[:, c:min(c + LANES, win)] for c in range(0, win, LANES)]
            m_new = jnp.broadcast_to(jnp.max(functools.reduce(jnp.maximum, slabs), axis=-1, keepdims=True),
                                     (tq, LANES))
            if not first:
                m_old = m_ref[qsl, :]
                m_new = jnp.maximum(m_old, m_new)
            e_slabs = [jnp.exp(sl - m_new[:, :sl.shape[1]]) for sl in slabs]
            l_new = jnp.broadcast_to(jnp.sum(functools.reduce(jnp.add, e_slabs), axis=-1, keepdims=True),
                                     (tq, LANES))
            acc = _dot(jnp.concatenate(e_slabs, axis=1), v_ref[ksl, :])
            if not first:
                corr = jnp.exp(m_old - m_new)
                l_new = l_new + corr * l_ref[qsl, :]
                acc = acc + corr * acc_ref[qsl, :]
            if final:
                return qsl, (acc / l_new * _silu(z_ref[qsl, :])).astype(o_ref.dtype), None, None
            return qsl, acc, m_new, l_new

        def body(trip, carry, per_trip=per_trip, final=final, scores=scores, tile=tile):
            staged = [scores(trip * per_trip + t) for t in range(per_trip)]
            done = [tile(*st) for st in staged]
            for qsl, main, m_new, l_new in done:
                if final:
                    o_ref[qsl, :] = main
                else:
                    acc_ref[qsl, :] = main
                    m_ref[qsl, :] = m_new
                    l_ref[qsl, :] = l_new
            return carry

        lax.fori_loop(0, n_it // per_trip, body, 0)


def _dilated_attention(qk, v, z):
    seq, width = v.shape
    heads = width // HEAD_DIM
    col = pl.BlockSpec((seq, HEAD_DIM), lambda h: (0, h))
    return pl.pallas_call(
        functools.partial(_att_kernel, seq=seq),
        grid=(heads,),
        in_specs=[col, pl.BlockSpec((seq, HEAD_DIM), lambda h: (0, heads + h)), col, col],
        out_specs=col,
        out_shape=jax.ShapeDtypeStruct((seq, width), BF16),
        scratch_shapes=[pltpu.VMEM((seq, HEAD_DIM), F32), pltpu.VMEM((seq, LANES), F32),
                        pltpu.VMEM((seq, LANES), F32),
                        pltpu.VMEM((ATT_OFFSETS, ATT_TQ, ATT_TQ + 2 * ATT_RADIUS), F32)],
        compiler_params=_params(("parallel",)),
        name="dilated_attention",
    )(qk, qk, v, z)


def _gdn_out_kernel(of_ref, ob_ref, z_ref, w_ref, y_ref):
    w = w_ref[...]
    for h in range(of_ref.shape[1] // HEAD_DIM):
        sl = slice(h * HEAD_DIM, (h + 1) * HEAD_DIM)
        o = of_ref[:, sl] + ob_ref[:, sl]
        inv = lax.rsqrt(jnp.mean(o * o, axis=-1, keepdims=True) + RMS_EPS)
        y_ref[:, sl] = (o * inv * w * z_ref[:, sl]).astype(y_ref.dtype)


def _gdn_out(o_f, o_b, z_silu, norm_w):
    seq, width = o_f.shape
    tm = min(512, seq)
    blk = pl.BlockSpec((tm, width), lambda i: (i, 0))
    return pl.pallas_call(
        _gdn_out_kernel,
        grid=(seq // tm,),
        in_specs=[blk, blk, blk, pl.BlockSpec((1, HEAD_DIM), lambda i: (0, 0))],
        out_specs=blk,
        out_shape=jax.ShapeDtypeStruct((seq, width), BF16),
        compiler_params=_params(("parallel",)),
        name="gdn_norm_gate",
    )(o_f, o_b, z_silu, norm_w)


def _layer_norm_rows(pre_ref, g_ref, b_ref, out_refs, n_col, tn):
    width = n_col * tn
    total = jnp.zeros((pre_ref.shape[1], 1), F32)
    for c in range(n_col):
        total = total + jnp.sum(pre_ref[c], axis=-1, keepdims=True)
    mu = total / width
    sq = jnp.zeros_like(total)
    for c in range(n_col):
        dev = pre_ref[c] - mu
        sq = sq + jnp.sum(dev * dev, axis=-1, keepdims=True)
    inv = lax.rsqrt(sq / width + LN_EPS)
    for c in range(n_col):
        sl = slice(c * tn, (c + 1) * tn)
        y = (pre_ref[c] - mu) * inv * g_ref[:, sl] + b_ref[:, sl]
        for ref in out_refs:
            ref[:, sl] = y.astype(ref.dtype)


def _outproj_kernel(ya_ref, yb_ref, wa_ref, wb_ref, x_ref, g_ref, b_ref, h_ref, pre_ref, *, n_col, tn, alpha):
    n = pl.program_id(1)
    mix = (jnp.dot(ya_ref[...], wa_ref[...], preferred_element_type=F32)
           + jnp.dot(yb_ref[...], wb_ref[...], preferred_element_type=F32))
    pre_ref[n] = alpha * x_ref[...] + mix

    @pl.when(n == n_col - 1)
    def _():
        _layer_norm_rows(pre_ref, g_ref, b_ref, (h_ref,), n_col, tn)


def _out_projection(y_a, y_b, w_a, w_b, x, gamma, beta, alpha, tm=512, tn=512):
    seq, d_model = x.shape
    tm = min(tm, seq)
    n_col = d_model // tn
    ka, kb = y_a.shape[1], y_b.shape[1]
    row = lambda i, n: (i, 0)
    full = pl.BlockSpec((tm, d_model), row)
    vec = pl.BlockSpec((1, d_model), lambda i, n: (0, 0))
    return pl.pallas_call(
        functools.partial(_outproj_kernel, n_col=n_col, tn=tn, alpha=alpha),
        grid=(seq // tm, n_col),
        in_specs=[pl.BlockSpec((tm, ka), row), pl.BlockSpec((tm, kb), row),
                  pl.BlockSpec((ka, tn), lambda i, n: (0, n)), pl.BlockSpec((kb, tn), lambda i, n: (0, n)),
                  pl.BlockSpec((tm, tn), lambda i, n: (i, n)), vec, vec],
        out_specs=full,
        out_shape=jax.ShapeDtypeStruct((seq, d_model), F32),
        scratch_shapes=[pltpu.VMEM((n_col, tm, tn), F32)],
        compiler_params=_params(("parallel", "arbitrary")),
        name="out_proj_ln",
    )(y_a, y_b, w_a, w_b, x, gamma, beta)


def _ple_kernel(hb_ref, wg_ref, p_ref, wu_ref, h_ref, g_ref, b_ref, o_ref, pre_ref, *, n_col, tn, alpha):
    n = pl.program_id(1)
    gate = _sigmoid(jnp.dot(hb_ref[...], wg_ref[...], preferred_element_type=F32))
    up = jnp.dot(p_ref[...], wu_ref[...], preferred_element_type=F32)
    pre_ref[n] = alpha * h_ref[...] + gate * up

    @pl.when(n == n_col - 1)
    def _():
        _layer_norm_rows(pre_ref, g_ref, b_ref, (o_ref,), n_col, tn)


def _ple_layer(h_bf, w_gate, p_bf, w_up, h, gamma, beta, alpha, tm=512, tn=512):
    seq, d_model = h.shape
    tm = min(tm, seq)
    n_col = d_model // tn
    pd = p_bf.shape[1]
    row = lambda i, n: (i, 0)
    vec = pl.BlockSpec((1, d_model), lambda i, n: (0, 0))
    return pl.pallas_call(
        functools.partial(_ple_kernel, n_col=n_col, tn=tn, alpha=alpha),
        grid=(seq // tm, n_col),
        in_specs=[pl.BlockSpec((tm, d_model), row), pl.BlockSpec((d_model, tn), lambda i, n: (0, n)),
                  pl.BlockSpec((tm, pd), row), pl.BlockSpec((pd, tn), lambda i, n: (0, n)),
                  pl.BlockSpec((tm, tn), lambda i, n: (i, n)), vec, vec],
        out_specs=pl.BlockSpec((tm, d_model), row),
        out_shape=jax.ShapeDtypeStruct((seq, d_model), F32),
        scratch_shapes=[pltpu.VMEM((n_col, tm, tn), F32)],
        compiler_params=_params(("parallel", "arbitrary")),
        name="ple_ln",
    )(h_bf, w_gate, p_bf, w_up, h, gamma, beta)


def _rotary_tables(seq):
    half = HEAD_DIM // 2
    inv_freq = 1.0 / (jnp.float32(ROPE_THETA) ** (jnp.arange(half, dtype=F32) / half))
    ang = jnp.arange(seq).astype(F32)[:, None] * inv_freq[None, :]
    cos, sin = jnp.cos(ang), jnp.sin(ang)
    return jnp.concatenate([cos, cos], axis=-1), jnp.concatenate([-sin, sin], axis=-1)


def _layer(h, p, w_in, conv_w, a_log, dt_bias, gdn_norm_w, w_out, ln1_g, ln1_b, w_ple_gate, w_ple_up,
           ln2_g, ln2_b, alpha):
    seq, d_model = h.shape
    key_dim = GDN_K_HEADS * HEAD_DIM
    d_gdn = GDN_V_HEADS * HEAD_DIM
    d_att = ATT_HEADS * HEAD_DIM
    n_gate = 2 * GDN_V_HEADS
    o_qkv = 0
    o_za = 2 * key_dim + d_gdn
    o_ab = o_za + d_gdn
    o_qk = o_ab + 2 * n_gate
    o_vb = o_qk + 2 * d_att
    o_zb = o_vb + d_att

    x_bf = h.astype(BF16)
    w = lambda lo, hi: w_in[:, lo:hi].astype(BF16)

    qkv_a = _project(x_bf, w(o_qkv, o_za), "proj_gdn_qkv")
    za = _project(x_bf, w(o_za, o_ab), "proj_gdn_gate", _epi_silu)
    w_ab = jnp.pad(w(o_ab, o_qk), ((0, 0), (0, LANES - 2 * n_gate)))
    pad_vec = lambda t: jnp.pad(t.reshape(1, n_gate).astype(F32), ((0, 0), (0, LANES - n_gate)))
    vec_spec = pl.BlockSpec((1, LANES), lambda j, i: (0, 0))
    gates = _project(x_bf, w_ab, "proj_gdn_decay", _epi_gates, extra=(pad_vec(a_log), pad_vec(dt_bias)),
                     extra_specs=(vec_spec, vec_spec))
    cos, sin = _rotary_tables(seq)
    tm_rot = min(512, seq)
    tab_spec = pl.BlockSpec((tm_rot, HEAD_DIM), lambda j, i: (i, 0))
    tn_rot = 1024
    qk_b = _project(x_bf, w(o_qk, o_vb), "proj_att_qk",
                    functools.partial(_epi_rotary, n_q_blocks=d_att // tn_rot),
                    extra=(cos, sin), extra_specs=(tab_spec, tab_spec), tm=tm_rot, tn=tn_rot)
    v_b = _project(x_bf, w(o_vb, o_zb), "proj_att_v")
    z_b = _project(x_bf, w(o_zb, o_zb + d_att), "proj_att_gate")

    conv_w8 = jnp.pad(conv_w.astype(F32), ((0, SUBLANES - CONV_WIDTH), (0, 0)))
    qkv_n = _conv_norm(qkv_a, conv_w8)

    o_f, o_b = _gated_delta(qkv_n, gates)

    y_gdn = _gdn_out(o_f, o_b, za, gdn_norm_w.reshape(1, HEAD_DIM).astype(F32))

    y_att = _dilated_attention(qk_b, v_b, z_b)

    w_o = w_out.astype(BF16)
    h1 = _out_projection(y_gdn, y_att, w_o[:d_gdn], w_o[d_gdn:], h,
                         ln1_g.reshape(1, d_model), ln1_b.reshape(1, d_model), alpha)

    return _ple_layer(h1.astype(BF16), w_ple_gate.astype(BF16), p.astype(BF16), w_ple_up.astype(BF16), h1,
                      ln2_g.reshape(1, d_model), ln2_b.reshape(1, d_model), alpha)


def kernel(x, p, w_in, conv_w, a_log, dt_bias, gdn_norm_w, w_out, ln1_g, ln1_b, w_ple_gate, w_ple_up, ln2_g, ln2_b):
    batch = x.shape[0]
    depth = w_in.shape[0]
    alpha = (2 * depth) ** 0.25
    outs = []
    for b in range(batch):
        h = x[b]
        for i in range(depth):
            h = _layer(h, p[i, b], w_in[i], conv_w[i], a_log[i], dt_bias[i], gdn_norm_w[i], w_out[i],
                       ln1_g[i], ln1_b[i], w_ple_gate[i], w_ple_up[i], ln2_g[i], ln2_b[i], alpha)
        outs.append(h)
    return jnp.stack(outs)
```

```python
import functools

import jax
import jax.numpy as jnp
from jax import lax
from jax.experimental import pallas as pl
from jax.experimental.pallas import tpu as pltpu

HEAD_DIM = 128
GDN_V_HEADS = 16
GDN_K_HEADS = 8
ATT_HEADS = 16
CONV_WIDTH = 5
GDN_CHUNK = 64
DILATIONS = (16, 4, 1)
ATT_RADIUS = 64
ROPE_THETA = 10000.0
LN_EPS = 1e-5
RMS_EPS = 1e-6
NEG_INF = -1e30
Q_SCALE = HEAD_DIM ** -0.5

LANES = 128
SUBLANES = 8
VMEM_LIMIT = 56 * 1024 * 1024

F32 = jnp.float32
BF16 = jnp.bfloat16
HI = lax.Precision.HIGHEST


def _params(sem, vmem=VMEM_LIMIT):
    return pltpu.CompilerParams(dimension_semantics=sem, vmem_limit_bytes=vmem)


def _sigmoid(x):
    return 1.0 / (1.0 + jnp.exp(-x))


def _silu(x):
    return x * _sigmoid(x)


def _dot(a, b):
    return jnp.dot(a.astype(BF16), b.astype(BF16), preferred_element_type=F32)


def _dot_nt(a, b):
    return lax.dot_general(a.astype(BF16), b.astype(BF16), (((1,), (1,)), ((), ())),
                           preferred_element_type=F32)


def _dot_tn(a, b):
    return lax.dot_general(a.astype(BF16), b.astype(BF16), (((0,), (0,)), ((), ())),
                           preferred_element_type=F32)


def _dot_hi(a, b):
    return jnp.dot(a, b, preferred_element_type=F32, precision=HI)


def _epi_none(acc, j):
    return acc


def _epi_silu(acc, j):
    return _silu(acc)


def _epi_rotary(acc, j, cos_ref, sin_ref, *, n_q_blocks):
    cos = cos_ref[...]
    sin = sin_ref[...]
    scale = jnp.where(j < n_q_blocks, Q_SCALE, 1.0).astype(F32)
    outs = []
    for c in range(acc.shape[1] // HEAD_DIM):
        t = acc[:, c * HEAD_DIM:(c + 1) * HEAD_DIM]
        outs.append((t * cos + pltpu.roll(t, HEAD_DIM // 2, axis=1) * sin) * scale)
    return jnp.concatenate(outs, axis=1)


def _epi_gates(acc, j, alog_ref, dtb_ref):
    t = acc + dtb_ref[...]
    softplus = jnp.maximum(t, 0.0) + jnp.log1p(jnp.exp(-jnp.abs(t)))
    g = -jnp.exp(alog_ref[...]) * softplus
    beta = _sigmoid(acc)
    lane = lax.broadcasted_iota(jnp.int32, acc.shape, 1)
    n_gate = 2 * GDN_V_HEADS
    return jnp.where(lane < n_gate, g, jnp.where(lane < 2 * n_gate, beta, 0.0))


def _mm_kernel(x_ref, w_ref, *refs, epilogue):
    o_ref = refs[-1]
    acc = lax.dot_general(x_ref[...], w_ref[...], (((1,), (1,)), ((), ())), preferred_element_type=F32)
    o_ref[...] = epilogue(acc, pl.program_id(0), *refs[:-1]).astype(o_ref.dtype)


def _project(x, w_t, row0, n, name, epilogue=_epi_none, extra=(), extra_specs=(), out_dtype=F32, tm=512, tn=1024):
    m, kd = x.shape
    tm, tn = min(tm, m), min(tn, n)
    align = 2 * SUBLANES
    assert m % tm == 0 and n % tn == 0 and row0 + n <= w_t.shape[0] and row0 % align == 0 and tn % align == 0
    return pl.pallas_call(
        functools.partial(_mm_kernel, epilogue=epilogue),
        grid=(n // tn, m // tm),
        in_specs=[pl.BlockSpec((tm, kd), lambda j, i: (i, 0)),
                  pl.BlockSpec((pl.Element(tn), pl.Element(kd)),
                               lambda j, i: (pl.multiple_of(row0 + j * tn, align), 0)), *extra_specs],
        out_specs=pl.BlockSpec((tm, tn), lambda j, i: (i, j)),
        out_shape=jax.ShapeDtypeStruct((m, n), out_dtype),
        compiler_params=_params(("parallel", "parallel")),
        name=name,
    )(x, w_t, *extra)


def _conv_kernel(u_ref, w_ref, o_ref, *, seq, tile, n_q, n_k):
    j = pl.program_id(0)
    n_tiles = seq // tile
    w = w_ref[...]
    pad = (CONV_WIDTH - 1) // 2
    ext_rows = tile + 2 * SUBLANES

    def body(i, carry):
        t0 = pl.multiple_of(i * tile, tile)
        cur = u_ref[pl.ds(t0, tile), :]
        p0 = pl.multiple_of(jnp.maximum(t0 - SUBLANES, 0), SUBLANES)
        n0 = pl.multiple_of(jnp.minimum(t0 + tile, seq - SUBLANES), SUBLANES)
        prev = jnp.where(i > 0, u_ref[pl.ds(p0, SUBLANES), :], 0.0)
        nxt = jnp.where(i < n_tiles - 1, u_ref[pl.ds(n0, SUBLANES), :], 0.0)
        ext = jnp.concatenate([prev, cur, nxt], axis=0)
        acc = jnp.zeros((tile, LANES), F32)
        for tap in range(CONV_WIDTH):
            shift = (pad - tap) % ext_rows
            sh = ext if shift == 0 else pltpu.roll(ext, shift, axis=0)
            acc = acc + sh[SUBLANES:SUBLANES + tile] * w[tap:tap + 1, :]
        y = _silu(acc)
        inv = lax.rsqrt(jnp.sum(y * y, axis=-1, keepdims=True) + RMS_EPS)
        scale = jnp.where(j < n_q, inv * Q_SCALE, jnp.where(j < n_q + n_k, inv, 1.0))
        o_ref[pl.ds(t0, tile), :] = y * scale
        return carry

    lax.fori_loop(0, n_tiles, body, 0)


def _conv_norm(u, conv_w8):
    seq, chans = u.shape
    tile = min(256, seq)
    return pl.pallas_call(
        functools.partial(_conv_kernel, seq=seq, tile=tile, n_q=GDN_K_HEADS, n_k=GDN_K_HEADS),
        grid=(chans // LANES,),
        in_specs=[pl.BlockSpec((seq, LANES), lambda j: (0, j)),
                  pl.BlockSpec((SUBLANES, LANES), lambda j: (0, j))],
        out_specs=pl.BlockSpec((seq, LANES), lambda j: (0, j)),
        out_shape=jax.ShapeDtypeStruct((seq, chans), F32),
        compiler_params=_params(("parallel",)),
        name="gdn_conv_norm",
    )(u, conv_w8)


GDN_SUB = 16


def _pair_block_diag(x, left):
    xb = x.astype(BF16)
    zero = jnp.zeros_like(xb)
    return jnp.concatenate([jnp.where(left, xb, zero), jnp.where(left, zero, xb)], axis=0)


def _pair_dot(x, y, left):
    return jnp.dot(x.astype(BF16), _pair_block_diag(y, left), preferred_element_type=F32)


def _unit_tri_inverse_minus_eye(a_list, same_block, left):
    dot = functools.partial(_pair_dot, left=left)
    d = [jnp.where(same_block, a, 0.0) for a in a_list]
    off = [a - x for a, x in zip(a_list, d)]
    p = [-x for x in d]
    e = d
    for _ in range(3):
        e = [dot(x, x) for x in e]
        p = [x + y + dot(x, y) for x, y in zip(p, e)]
    m = [y + dot(x, y) for x, y in zip(p, off)]
    m2 = [dot(x, x) for x in m]
    q = [y - x - dot(x, y) for x, y in zip(m, m2)]
    return [x + y + dot(x, y) for x, y in zip(q, p)]


def _dot_split(a, b_bf):
    hi = a.astype(BF16)
    lo = (a - hi.astype(F32)).astype(BF16)
    return jnp.dot(hi, b_bf, preferred_element_type=F32) + jnp.dot(lo, b_bf, preferred_element_type=F32)


def _split_dot(a_bf, b):
    hi = b.astype(BF16)
    lo = (b - hi.astype(F32)).astype(BF16)
    return jnp.dot(a_bf, hi, preferred_element_type=F32) + jnp.dot(a_bf, lo, preferred_element_type=F32)


def _gdn_kernel(qf_ref, kf_ref, vf_ref, gf_ref, qb_ref, kb_ref, vb_ref, gb_ref,
                of_ref, ob_ref, st_ref, lhs_ref, psi_ref, oin_ref, *, n_chunks, chunk):
    j = pl.program_id(0)

    @pl.when(pl.program_id(1) == 0)
    def _():
        st_ref[...] = jnp.zeros_like(st_ref)

    n_e = 2
    width = n_e * chunk
    row = lax.broadcasted_iota(jnp.int32, (chunk, width), 0)
    lane = lax.broadcasted_iota(jnp.int32, (chunk, width), 1)
    col = lane % chunk
    left = lane < chunk
    same_block = (row // GDN_SUB) == (col // GDN_SUB)
    r2 = lax.broadcasted_iota(jnp.int32, (chunk, chunk), 0)
    c2 = lax.broadcasted_iota(jnp.int32, (chunk, chunk), 1)
    sel_row = lax.broadcasted_iota(jnp.int32, (LANES, LANES), 0)
    sel_lane = lax.broadcasted_iota(jnp.int32, (LANES, LANES), 1)
    dirs = ((qf_ref, kf_ref, vf_ref, gf_ref, of_ref), (qb_ref, kb_ref, vb_ref, gb_ref, ob_ref))

    a_list, qkd_list, rhs_list, tail_list, k_list, gamma_list, qdec_list = [], [], [], [], [], [], []
    for d, (q_ref, k_ref, v_ref, g_ref, _) in enumerate(dirs):
        backward = d == 1
        incl = (col >= row) if backward else (col <= row)
        strict = (col > row) if backward else (col < row)
        tri_c = ((c2 >= r2) if backward else (c2 <= r2)).astype(BF16)
        tri_r = ((row >= col) if backward else (row <= col)).astype(F32)
        last = 0 if backward else chunk - 1
        src = d * GDN_V_HEADS + n_e * j + jnp.where(sel_lane < n_e, sel_lane, 2 * GDN_V_HEADS + sel_lane - n_e)
        sel = ((sel_row == src) & (sel_lane < 2 * n_e)).astype(BF16)
        picked_all = _dot_split(g_ref[...], sel)
        for c in range(n_chunks):
            rows = slice(c * chunk, (c + 1) * chunk)
            q = q_ref[rows, :]
            k = k_ref[rows, :]
            picked = picked_all[rows]
            cum = _split_dot(tri_c, picked)
            rep = lambda t, i: jnp.broadcast_to(t[:, i:i + 1], (chunk, LANES))
            g_rep = [rep(picked, e) for e in range(n_e)]
            b_rep = [rep(picked, n_e + e) for e in range(n_e)]
            cum_rep = [rep(cum, e) for e in range(n_e)]
            g_pair = jnp.where(left, g_rep[0], g_rep[1])
            cum_col = jnp.where(left, cum_rep[0], cum_rep[1])
            cum_row = jnp.sum(g_pair * tri_r, axis=0, keepdims=True)
            decay = jnp.where(incl, jnp.exp(jnp.where(incl, cum_col - cum_row, 0.0)), 0.0)
            k2 = jnp.concatenate([k, k], axis=0)
            kk = _dot_nt(k, k2)
            qk = _dot_nt(q, k2)
            a_list.append(jnp.where(strict, kk * decay, 0.0) * jnp.where(left, b_rep[0], b_rep[1]))
            qkd_list.append(qk * decay)
            k_list.append(k)
            rhs, tail = [], []
            for e in range(n_e):
                v = v_ref[rows, e * HEAD_DIM:(e + 1) * HEAD_DIM]
                g_last = cum_rep[e][last:last + 1, :]
                eg = jnp.exp(cum_rep[e])
                rhs.append(jnp.concatenate([v * b_rep[e], k * (b_rep[e] * eg)], axis=1))
                tail.append(jnp.exp(g_last - cum_rep[e]))
                gamma_list.append(jnp.exp(g_last))
                qdec_list.append(q * eg)
            rhs_list.append(jnp.concatenate(rhs, axis=0))
            tail_list.append(jnp.concatenate(tail, axis=0))

    t_list = _unit_tri_inverse_minus_eye(a_list, same_block, left)
    uw_list = [r + jnp.dot(_pair_block_diag(t, left), r.astype(BF16), preferred_element_type=F32)
               for t, r in zip(t_list, rhs_list)]
    qo_list = [jnp.dot(_pair_block_diag(x, left), y.astype(BF16), preferred_element_type=F32)
               for x, y in zip(qkd_list, uw_list)]
    for dc in range(2 * n_chunks):
        uw, tail = uw_list[dc], tail_list[dc]
        scaled = jnp.concatenate([uw[e * chunk:(e + 1) * chunk, h * HEAD_DIM:(h + 1) * HEAD_DIM]
                                  * tail[e * chunk:(e + 1) * chunk] for e in range(n_e) for h in range(2)],
                                 axis=1)
        kt = _dot_tn(k_list[dc], scaled)
        for e in range(n_e):
            b = n_e * dc + e
            base = 2 * HEAD_DIM * e
            qo = qo_list[dc][e * chunk:(e + 1) * chunk]
            psi_ref[b] = kt[:, base:base + HEAD_DIM]
            lhs_ref[b, :HEAD_DIM, :] = (-kt[:, base + HEAD_DIM:base + 2 * HEAD_DIM]).astype(BF16)
            lhs_ref[b, HEAD_DIM:, :] = (qdec_list[b] - qo[:, HEAD_DIM:]).astype(BF16)
            oin_ref[b] = qo[:, :HEAD_DIM]

    states = [st_ref[i] for i in range(2 * n_e)]
    for step in range(n_chunks):
        for d in range(2):
            c = n_chunks - 1 - step if d == 1 else step
            o_ref = dirs[d][4]
            for e in range(n_e):
                b = n_e * (d * n_chunks + c) + e
                s = states[n_e * d + e]
                res = jnp.dot(lhs_ref[b], s.astype(BF16), preferred_element_type=F32)
                states[n_e * d + e] = gamma_list[b] * s + res[:HEAD_DIM] + psi_ref[b]
                o_ref[c * chunk:(c + 1) * chunk, e * HEAD_DIM:(e + 1) * HEAD_DIM] = res[HEAD_DIM:] + oin_ref[b]
    for i in range(2 * n_e):
        st_ref[i] = states[i]


def _gated_delta(qkv, gates, n_chunks=8):
    seq = qkv.shape[0]
    blk = n_chunks * GDN_CHUNK
    nb = seq // blk
    rep = GDN_V_HEADS // GDN_K_HEADS
    vw = rep * HEAD_DIM
    v_off = 2 * GDN_K_HEADS * HEAD_DIM // vw
    n_chains = 2 * n_chunks * rep

    def specs(rowmap):
        return [pl.BlockSpec((blk, HEAD_DIM), lambda j, n: (rowmap(n), j)),
                pl.BlockSpec((blk, HEAD_DIM), lambda j, n: (rowmap(n), GDN_K_HEADS + j)),
                pl.BlockSpec((blk, vw), lambda j, n: (rowmap(n), v_off + j)),
                pl.BlockSpec((blk, LANES), lambda j, n: (rowmap(n), 0))]

    fwd = lambda n: n
    bwd = lambda n: nb - 1 - n
    out_sd = jax.ShapeDtypeStruct((seq, GDN_V_HEADS * HEAD_DIM), F32)
    return pl.pallas_call(
        functools.partial(_gdn_kernel, n_chunks=n_chunks, chunk=GDN_CHUNK),
        grid=(GDN_K_HEADS, nb),
        in_specs=specs(fwd) + specs(bwd),
        out_specs=[pl.BlockSpec((blk, vw), lambda j, n: (fwd(n), j)),
                   pl.BlockSpec((blk, vw), lambda j, n: (bwd(n), j))],
        out_shape=[out_sd, out_sd],
        scratch_shapes=[pltpu.VMEM((2 * rep, HEAD_DIM, HEAD_DIM), F32),
                        pltpu.VMEM((n_chains, HEAD_DIM + GDN_CHUNK, HEAD_DIM), BF16),
                        pltpu.VMEM((n_chains, HEAD_DIM, HEAD_DIM), F32),
                        pltpu.VMEM((n_chains, GDN_CHUNK, HEAD_DIM), F32)],
        compiler_params=_params(("parallel", "arbitrary")),
        name="gdn_delta_rule",
    )(qkv, qkv, qkv, gates, qkv, qkv, qkv, gates)


ATT_TQ = 256
ATT_OFFSETS = 3


def _att_kernel(q_ref, k_ref, v_ref, z_ref, o_ref, acc_ref, m_ref, l_ref, band_ref, *, seq):
    rel = (lax.broadcasted_iota(jnp.int32, band_ref.shape[1:], 0)
           - lax.broadcasted_iota(jnp.int32, band_ref.shape[1:], 1))
    for i in range(ATT_OFFSETS):
        band_ref[i] = (jnp.abs(rel + i * ATT_RADIUS) <= ATT_RADIUS).astype(F32)

    for p, dil in enumerate(DILATIONS):
        length = seq // dil
        tq = min(ATT_TQ, length)
        win = min(tq + 2 * ATT_RADIUS, length)
        tiles = length // tq
        first, final = p == 0, p == len(DILATIONS) - 1
        n_it = dil * tiles
        per_trip = max(t for t in (4, 2, 1) if n_it % t == 0)

        def scores(it, dil=dil, tq=tq, win=win, tiles=tiles, length=length):
            r = it // tiles
            m0 = (it % tiles) * tq
            k0 = jnp.clip(m0 - ATT_RADIUS, 0, length - win)
            if dil == 1:
                qsl = pl.ds(pl.multiple_of(m0, tq), tq)
                ksl = pl.ds(pl.multiple_of(k0, ATT_RADIUS), win)
            else:
                qsl = pl.ds(r + dil * m0, tq, stride=dil)
                ksl = pl.ds(r + dil * k0, win, stride=dil)
            s = _dot_nt(q_ref[qsl, :], k_ref[ksl, :])
            if win == tq + 2 * ATT_RADIUS:
                valid = band_ref[(m0 - k0) // ATT_RADIUS, :tq, :win] > 0.0
            else:
                qpos = m0 + lax.broadcasted_iota(jnp.int32, (tq, win), 0)
                kpos = k0 + lax.broadcasted_iota(jnp.int32, (tq, win), 1)
                valid = jnp.abs(qpos - kpos) <= ATT_RADIUS
            return qsl, ksl, jnp.where(valid, s, NEG_INF)

        def tile(qsl, ksl, s, tq=tq, win=win, first=first, final=final):
            slabs = [s[:, c:min(c + LANES, win)] for c in range(0, win, LANES)]
            m_new = jnp.broadcast_to(jnp.max(functools.reduce(jnp.maximum, slabs), axis=-1, keepdims=True),
                                     (tq, LANES))
            if not first:
                m_old = m_ref[qsl, :]
                m_new = jnp.maximum(m_old, m_new)
            e_slabs = [jnp.exp(sl - m_new[:, :sl.shape[1]]) for sl in slabs]
            l_new = jnp.broadcast_to(jnp.sum(functools.reduce(jnp.add, e_slabs), axis=-1, keepdims=True),
                                     (tq, LANES))
            acc = _dot(jnp.concatenate(e_slabs, axis=1), v_ref[ksl, :])
            if not first:
                corr = jnp.exp(m_old - m_new)
                l_new = l_new + corr * l_ref[qsl, :]
                acc = acc + corr * acc_ref[qsl, :]
            if final:
                return qsl, (acc / l_new * _silu(z_ref[qsl, :])).astype(o_ref.dtype), None, None
            return qsl, acc, m_new, l_new

        def body(trip, carry, per_trip=per_trip, final=final, scores=scores, tile=tile):
            staged = [scores(trip * per_trip + t) for t in range(per_trip)]
            done = [tile(*st) for st in staged]
            for qsl, main, m_new, l_new in done:
                if final:
                    o_ref[qsl, :] = main
                else:
                    acc_ref[qsl, :] = main
                    m_ref[qsl, :] = m_new
                    l_ref[qsl, :] = l_new
            return carry

        lax.fori_loop(0, n_it // per_trip, body, 0)


def _dilated_attention(qk, v, z):
    seq, width = v.shape
    heads = width // HEAD_DIM
    col = pl.BlockSpec((seq, HEAD_DIM), lambda h: (0, h))
    return pl.pallas_call(
        functools.partial(_att_kernel, seq=seq),
        grid=(heads,),
        in_specs=[col, pl.BlockSpec((seq, HEAD_DIM), lambda h: (0, heads + h)), col, col],
        out_specs=col,
        out_shape=jax.ShapeDtypeStruct((seq, width), BF16),
        scratch_shapes=[pltpu.VMEM((seq, HEAD_DIM), F32), pltpu.VMEM((seq, LANES), F32),
                        pltpu.VMEM((seq, LANES), F32),
                        pltpu.VMEM((ATT_OFFSETS, ATT_TQ, ATT_TQ + 2 * ATT_RADIUS), F32)],
        compiler_params=_params(("parallel",)),
        name="dilated_attention",
    )(qk, qk, v, z)


def _gdn_out_kernel(of_ref, ob_ref, z_ref, w_ref, y_ref):
    w = w_ref[...]
    for h in range(of_ref.shape[1] // HEAD_DIM):
        sl = slice(h * HEAD_DIM, (h + 1) * HEAD_DIM)
        o = of_ref[:, sl] + ob_ref[:, sl]
        inv = lax.rsqrt(jnp.mean(o * o, axis=-1, keepdims=True) + RMS_EPS)
        y_ref[:, sl] = (o * inv * w * z_ref[:, sl]).astype(y_ref.dtype)


def _gdn_out(o_f, o_b, z_silu, norm_w):
    seq, width = o_f.shape
    tm = min(512, seq)
    blk = pl.BlockSpec((tm, width), lambda i: (i, 0))
    return pl.pallas_call(
        _gdn_out_kernel,
        grid=(seq // tm,),
        in_specs=[blk, blk, blk, pl.BlockSpec((1, HEAD_DIM), lambda i: (0, 0))],
        out_specs=blk,
        out_shape=jax.ShapeDtypeStruct((seq, width), BF16),
        compiler_params=_params(("parallel",)),
        name="gdn_norm_gate",
    )(o_f, o_b, z_silu, norm_w)


def _layer_norm_rows(pre_ref, g_ref, b_ref, out_refs, n_col, tn):
    width = n_col * tn
    total = jnp.zeros((pre_ref.shape[1], 1), F32)
    for c in range(n_col):
        total = total + jnp.sum(pre_ref[c], axis=-1, keepdims=True)
    mu = total / width
    sq = jnp.zeros_like(total)
    for c in range(n_col):
        dev = pre_ref[c] - mu
        sq = sq + jnp.sum(dev * dev, axis=-1, keepdims=True)
    inv = lax.rsqrt(sq / width + LN_EPS)
    for c in range(n_col):
        sl = slice(c * tn, (c + 1) * tn)
        y = (pre_ref[c] - mu) * inv * g_ref[:, sl] + b_ref[:, sl]
        for ref in out_refs:
            ref[:, sl] = y.astype(ref.dtype)


def _outproj_kernel(ya_ref, yb_ref, wa_ref, wb_ref, x_ref, g_ref, b_ref, h_ref, pre_ref, *, n_col, tn, alpha):
    n = pl.program_id(1)
    mix = (jnp.dot(ya_ref[...], wa_ref[...], preferred_element_type=F32)
           + jnp.dot(yb_ref[...], wb_ref[...], preferred_element_type=F32))
    pre_ref[n] = alpha * x_ref[...] + mix

    @pl.when(n == n_col - 1)
    def _():
        _layer_norm_rows(pre_ref, g_ref, b_ref, (h_ref,), n_col, tn)


def _out_projection(y_a, y_b, w, x, gamma, beta, alpha, tm=512, tn=512):
    seq, d_model = x.shape
    tm = min(tm, seq)
    n_col = d_model // tn
    ka, kb = y_a.shape[1], y_b.shape[1]
    assert ka == kb and w.shape[0] == ka + kb
    row = lambda i, n: (i, 0)
    full = pl.BlockSpec((tm, d_model), row)
    vec = pl.BlockSpec((1, d_model), lambda i, n: (0, 0))
    return pl.pallas_call(
        functools.partial(_outproj_kernel, n_col=n_col, tn=tn, alpha=alpha),
        grid=(seq // tm, n_col),
        in_specs=[pl.BlockSpec((tm, ka), row), pl.BlockSpec((tm, kb), row),
                  pl.BlockSpec((ka, tn), lambda i, n: (0, n)), pl.BlockSpec((kb, tn), lambda i, n: (1, n)),
                  pl.BlockSpec((tm, tn), lambda i, n: (i, n)), vec, vec],
        out_specs=full,
        out_shape=jax.ShapeDtypeStruct((seq, d_model), F32),
        scratch_shapes=[pltpu.VMEM((n_col, tm, tn), F32)],
        compiler_params=_params(("parallel", "arbitrary")),
        name="out_proj_ln",
    )(y_a, y_b, w, w, x, gamma, beta)


def _ple_kernel(hb_ref, wg_ref, p_ref, wu_ref, h_ref, g_ref, b_ref, o_ref, pre_ref, *, n_col, tn, alpha):
    n = pl.program_id(1)
    gate = _sigmoid(jnp.dot(hb_ref[...], wg_ref[...], preferred_element_type=F32))
    up = jnp.dot(p_ref[...], wu_ref[...], preferred_element_type=F32)
    pre_ref[n] = alpha * h_ref[...] + gate * up

    @pl.when(n == n_col - 1)
    def _():
        _layer_norm_rows(pre_ref, g_ref, b_ref, (o_ref,), n_col, tn)


def _ple_layer(h_bf, w_gate, p_bf, w_up, h, gamma, beta, alpha, tm=512, tn=512):
    seq, d_model = h.shape
    tm = min(tm, seq)
    n_col = d_model // tn
    pd = p_bf.shape[1]
    row = lambda i, n: (i, 0)
    vec = pl.BlockSpec((1, d_model), lambda i, n: (0, 0))
    return pl.pallas_call(
        functools.partial(_ple_kernel, n_col=n_col, tn=tn, alpha=alpha),
        grid=(seq // tm, n_col),
        in_specs=[pl.BlockSpec((tm, d_model), row), pl.BlockSpec((d_model, tn), lambda i, n: (0, n)),
                  pl.BlockSpec((tm, pd), row), pl.BlockSpec((pd, tn), lambda i, n: (0, n)),
                  pl.BlockSpec((tm, tn), lambda i, n: (i, n)), vec, vec],
        out_specs=pl.BlockSpec((tm, d_model), row),
        out_shape=jax.ShapeDtypeStruct((seq, d_model), F32),
        scratch_shapes=[pltpu.VMEM((n_col, tm, tn), F32)],
        compiler_params=_params(("parallel", "arbitrary")),
        name="ple_ln",
    )(h_bf, w_gate, p_bf, w_up, h, gamma, beta)


def _rotary_tables(seq):
    half = HEAD_DIM // 2
    inv_freq = 1.0 / (jnp.float32(ROPE_THETA) ** (jnp.arange(half, dtype=F32) / half))
    ang = jnp.arange(seq).astype(F32)[:, None] * inv_freq[None, :]
    cos, sin = jnp.cos(ang), jnp.sin(ang)
    return jnp.concatenate([cos, cos], axis=-1), jnp.concatenate([-sin, sin], axis=-1)


def _layer(h, p, w_in, conv_w, a_log, dt_bias, gdn_norm_w, w_out, ln1_g, ln1_b, w_ple_gate, w_ple_up,
           ln2_g, ln2_b, alpha):
    seq, d_model = h.shape
    key_dim = GDN_K_HEADS * HEAD_DIM
    d_gdn = GDN_V_HEADS * HEAD_DIM
    d_att = ATT_HEADS * HEAD_DIM
    n_gate = 2 * GDN_V_HEADS
    o_qkv = 0
    o_za = 2 * key_dim + d_gdn
    o_ab = o_za + d_gdn
    o_qk = o_ab + 2 * n_gate
    o_vb = o_qk + 2 * d_att
    o_zb = o_vb + d_att

    x_bf = h.astype(BF16)
    w_t = jnp.swapaxes(w_in, 0, 1)
    w_bf = w_t.astype(BF16)

    qkv_a = _project(x_bf, w_bf, o_qkv, o_za - o_qkv, "proj_gdn_qkv")
    za = _project(x_bf, w_bf, o_za, d_gdn, "proj_gdn_gate", _epi_silu)
    w_ab = jnp.pad(w_bf[o_ab:o_qk], ((0, LANES - 2 * n_gate), (0, 0)))
    pad_vec = lambda t: jnp.pad(t.reshape(1, n_gate).astype(F32), ((0, 0), (0, LANES - n_gate)))
    vec_spec = pl.BlockSpec((1, LANES), lambda j, i: (0, 0))
    gates = _project(x_bf, w_ab, 0, LANES, "proj_gdn_decay", _epi_gates,
                     extra=(pad_vec(a_log), pad_vec(dt_bias)), extra_specs=(vec_spec, vec_spec))
    cos, sin = _rotary_tables(seq)
    tm_rot = min(512, seq)
    tab_spec = pl.BlockSpec((tm_rot, HEAD_DIM), lambda j, i: (i, 0))
    tn_rot = 1024
    qk_b = _project(x_bf, w_bf, o_qk, 2 * d_att, "proj_att_qk",
                    functools.partial(_epi_rotary, n_q_blocks=d_att // tn_rot),
                    extra=(cos, sin), extra_specs=(tab_spec, tab_spec), tm=tm_rot, tn=tn_rot)
    v_b = _project(x_bf, w_bf, o_vb, d_att, "proj_att_v")
    z_b = _project(x_bf, w_bf, o_zb, d_att, "proj_att_gate")

    conv_w8 = jnp.pad(conv_w.astype(F32), ((0, SUBLANES - CONV_WIDTH), (0, 0)))
    qkv_n = _conv_norm(qkv_a, conv_w8)

    o_f, o_b = _gated_delta(qkv_n, gates)

    y_gdn = _gdn_out(o_f, o_b, za, gdn_norm_w.reshape(1, HEAD_DIM).astype(F32))

    y_att = _dilated_attention(qk_b, v_b, z_b)

    h1 = _out_projection(y_gdn, y_att, w_out.astype(BF16), h,
                         ln1_g.reshape(1, d_model), ln1_b.reshape(1, d_model), alpha)

    return _ple_layer(h1.astype(BF16), w_ple_gate.astype(BF16), p.astype(BF16), w_ple_up.astype(BF16), h1,
                      ln2_g.reshape(1, d_model), ln2_b.reshape(1, d_model), alpha)


def kernel(x, p, w_in, conv_w, a_log, dt_bias, gdn_norm_w, w_out, ln1_g, ln1_b, w_ple_gate, w_ple_up, ln2_g, ln2_b):
    batch = x.shape[0]
    depth = w_in.shape[0]
    alpha = (2 * depth) ** 0.25
    outs = []
    for b in range(batch):
        h = x[b]
        for i in range(depth):
            h = _layer(h, p[i, b], w_in[i], conv_w[i], a_log[i], dt_bias[i], gdn_norm_w[i], w_out[i],
                       ln1_g[i], ln1_b[i], w_ple_gate[i], w_ple_up[i], ln2_g[i], ln2_b[i], alpha)
        outs.append(h)
    return jnp.stack(outs)
```

```python
import functools

import jax
import jax.numpy as jnp
from jax import lax
from jax.experimental import pallas as pl
from jax.experimental.pallas import tpu as pltpu

HEAD_DIM = 128
GDN_V_HEADS = 16
GDN_K_HEADS = 8
ATT_HEADS = 16
CONV_WIDTH = 5
GDN_CHUNK = 64
DILATIONS = (16, 4, 1)
ATT_RADIUS = 64
ROPE_THETA = 10000.0
LN_EPS = 1e-5
RMS_EPS = 1e-6
NEG_INF = -1e30
Q_SCALE = HEAD_DIM ** -0.5

LANES = 128
SUBLANES = 8
VMEM_LIMIT = 56 * 1024 * 1024

F32 = jnp.float32
BF16 = jnp.bfloat16
HI = lax.Precision.HIGHEST


def _params(sem, vmem=VMEM_LIMIT):
    return pltpu.CompilerParams(dimension_semantics=sem, vmem_limit_bytes=vmem)


def _sigmoid(x):
    return 1.0 / (1.0 + jnp.exp(-x))


def _silu(x):
    return x * _sigmoid(x)


def _dot(a, b):
    return jnp.dot(a.astype(BF16), b.astype(BF16), preferred_element_type=F32)


def _dot_nt(a, b):
    return lax.dot_general(a.astype(BF16), b.astype(BF16), (((1,), (1,)), ((), ())),
                           preferred_element_type=F32)


def _dot_tn(a, b):
    return lax.dot_general(a.astype(BF16), b.astype(BF16), (((0,), (0,)), ((), ())),
                           preferred_element_type=F32)


def _dot_hi(a, b):
    return jnp.dot(a, b, preferred_element_type=F32, precision=HI)


def _epi_none(acc, j):
    return acc


def _epi_silu(acc, j):
    return _silu(acc)


def _epi_rotary(acc, j, cos_ref, sin_ref, *, n_q_blocks):
    cos = cos_ref[...]
    sin = sin_ref[...]
    scale = jnp.where(j < n_q_blocks, Q_SCALE, 1.0).astype(F32)
    outs = []
    for c in range(acc.shape[1] // HEAD_DIM):
        t = acc[:, c * HEAD_DIM:(c + 1) * HEAD_DIM]
        outs.append((t * cos + pltpu.roll(t, HEAD_DIM // 2, axis=1) * sin) * scale)
    return jnp.concatenate(outs, axis=1)


def _epi_gates(acc, j, alog_ref, dtb_ref):
    t = acc + dtb_ref[...]
    softplus = jnp.maximum(t, 0.0) + jnp.log1p(jnp.exp(-jnp.abs(t)))
    g = -jnp.exp(alog_ref[...]) * softplus
    beta = _sigmoid(acc)
    lane = lax.broadcasted_iota(jnp.int32, acc.shape, 1)
    n_gate = 2 * GDN_V_HEADS
    return jnp.where(lane < n_gate, g, jnp.where(lane < 2 * n_gate, beta, 0.0))


def _mm_kernel(x_ref, w_ref, *refs, epilogue):
    o_ref = refs[-1]
    acc = lax.dot_general(x_ref[...], w_ref[...], (((1,), (1,)), ((), ())), preferred_element_type=F32)
    o_ref[...] = epilogue(acc, pl.program_id(0), *refs[:-1]).astype(o_ref.dtype)


def _project(x, w_t, row0, n, name, epilogue=_epi_none, extra=(), extra_specs=(), out_dtype=F32, tm=1024, tn=1024):
    m, kd = x.shape
    tm, tn = min(tm, m), min(tn, n)
    align = 2 * SUBLANES
    assert m % tm == 0 and n % tn == 0 and row0 + n <= w_t.shape[0] and row0 % align == 0 and tn % align == 0
    return pl.pallas_call(
        functools.partial(_mm_kernel, epilogue=epilogue),
        grid=(n // tn, m // tm),
        in_specs=[pl.BlockSpec((tm, kd), lambda j, i: (i, 0)),
                  pl.BlockSpec((pl.Element(tn), pl.Element(kd)),
                               lambda j, i: (pl.multiple_of(row0 + j * tn, align), 0)), *extra_specs],
        out_specs=pl.BlockSpec((tm, tn), lambda j, i: (i, j)),
        out_shape=jax.ShapeDtypeStruct((m, n), out_dtype),
        compiler_params=_params(("parallel", "parallel")),
        name=name,
    )(x, w_t, *extra)


def _conv_kernel(u_ref, w_ref, o_ref, *, seq, tile, n_q, n_k):
    j = pl.program_id(0)
    n_tiles = seq // tile
    w = w_ref[...]
    pad = (CONV_WIDTH - 1) // 2
    ext_rows = tile + 2 * SUBLANES

    def finish(t0, acc):
        y = _silu(acc)
        inv = lax.rsqrt(jnp.sum(y * y, axis=-1, keepdims=True) + RMS_EPS)
        scale = jnp.where(j < n_q, inv * Q_SCALE, jnp.where(j < n_q + n_k, inv, 1.0))
        o_ref[pl.ds(t0, tile), :] = y * scale

    def edge(i):
        t0 = i * tile
        zeros = jnp.zeros((SUBLANES, LANES), F32)
        prev = u_ref[t0 - SUBLANES:t0, :] if i > 0 else zeros
        nxt = u_ref[t0 + tile:t0 + tile + SUBLANES, :] if i < n_tiles - 1 else zeros
        ext = jnp.concatenate([prev, u_ref[t0:t0 + tile, :], nxt], axis=0)
        acc = jnp.zeros((tile, LANES), F32)
        for tap in range(CONV_WIDTH):
            shift = (pad - tap) % ext_rows
            sh = ext if shift == 0 else pltpu.roll(ext, shift, axis=0)
            acc = acc + sh[SUBLANES:SUBLANES + tile] * w[tap:tap + 1, :]
        finish(t0, acc)

    def body(i, carry):
        t0 = pl.multiple_of(i * tile, tile)
        acc = jnp.zeros((tile, LANES), F32)
        for tap in range(CONV_WIDTH):
            acc = acc + u_ref[pl.ds(t0 + (tap - pad), tile), :] * w[tap:tap + 1, :]
        finish(t0, acc)
        return carry

    edge(0)
    if n_tiles > 1:
        lax.fori_loop(1, n_tiles - 1, body, 0)
        edge(n_tiles - 1)


def _conv_norm(u, conv_w8):
    seq, chans = u.shape
    tile = min(256, seq)
    return pl.pallas_call(
        functools.partial(_conv_kernel, seq=seq, tile=tile, n_q=GDN_K_HEADS, n_k=GDN_K_HEADS),
        grid=(chans // LANES,),
        in_specs=[pl.BlockSpec((seq, LANES), lambda j: (0, j)),
                  pl.BlockSpec((SUBLANES, LANES), lambda j: (0, j))],
        out_specs=pl.BlockSpec((seq, LANES), lambda j: (0, j)),
        out_shape=jax.ShapeDtypeStruct((seq, chans), F32),
        compiler_params=_params(("parallel",)),
        name="gdn_conv_norm",
    )(u, conv_w8)


GDN_SUB = 16


def _lane_block_diag(x, lane_block):
    xb = x.astype(BF16)
    zero = jnp.zeros_like(xb)
    n_blocks = x.shape[1] // x.shape[0]
    return jnp.concatenate([jnp.where(lane_block == r, xb, zero) for r in range(n_blocks)], axis=0)


def _packed_dot(x, y, lane_block):
    return jnp.dot(x.astype(BF16), _lane_block_diag(y, lane_block), preferred_element_type=F32)


def _unit_tri_inverse_minus_eye(a_list, same_block, lane_block):
    dot = functools.partial(_packed_dot, lane_block=lane_block)
    d = [jnp.where(same_block, a, 0.0) for a in a_list]
    off = [a - x for a, x in zip(a_list, d)]
    p = [-x for x in d]
    e = d
    for _ in range(3):
        e = [dot(x, x) for x in e]
        p = [x + y + dot(x, y) for x, y in zip(p, e)]
    m = [y + dot(x, y) for x, y in zip(p, off)]
    m2 = [dot(x, x) for x in m]
    q = [y - x - dot(x, y) for x, y in zip(m, m2)]
    return [x + y + dot(x, y) for x, y in zip(q, p)]


def _dot_split(a, b_bf):
    hi = a.astype(BF16)
    lo = (a - hi.astype(F32)).astype(BF16)
    return jnp.dot(hi, b_bf, preferred_element_type=F32) + jnp.dot(lo, b_bf, preferred_element_type=F32)


def _split_dot(a_bf, b):
    hi = b.astype(BF16)
    lo = (b - hi.astype(F32)).astype(BF16)
    return jnp.dot(a_bf, hi, preferred_element_type=F32) + jnp.dot(a_bf, lo, preferred_element_type=F32)


def _gdn_kernel(qf_ref, kf_ref, vf_ref, gf_ref, qb_ref, kb_ref, vb_ref, gb_ref,
                of_ref, ob_ref, st_ref, lhs_ref, psi_ref, oin_ref, *, n_chunks, chunk):
    j = pl.program_id(0)

    @pl.when(pl.program_id(1) == 0)
    def _():
        st_ref[...] = jnp.zeros_like(st_ref)

    n_e = 2
    width = n_e * chunk
    row = lax.broadcasted_iota(jnp.int32, (chunk, width), 0)
    lane = lax.broadcasted_iota(jnp.int32, (chunk, width), 1)
    col = lane % chunk
    left = lane < chunk
    same_block = (row // GDN_SUB) == (col // GDN_SUB)
    r2 = lax.broadcasted_iota(jnp.int32, (chunk, chunk), 0)
    c2 = lax.broadcasted_iota(jnp.int32, (chunk, chunk), 1)
    sel_row = lax.broadcasted_iota(jnp.int32, (LANES, LANES), 0)
    sel_lane = lax.broadcasted_iota(jnp.int32, (LANES, LANES), 1)
    dirs = ((qf_ref, kf_ref, vf_ref, gf_ref, of_ref), (qb_ref, kb_ref, vb_ref, gb_ref, ob_ref))

    a_list, qkd_list, rhs_list, tail_list, k_list, gamma_list, qdec_list = [], [], [], [], [], [], []
    for d, (q_ref, k_ref, v_ref, g_ref, _) in enumerate(dirs):
        backward = d == 1
        incl = (col >= row) if backward else (col <= row)
        strict = (col > row) if backward else (col < row)
        tri_c = ((c2 >= r2) if backward else (c2 <= r2)).astype(BF16)
        tri_r = ((row >= col) if backward else (row <= col)).astype(F32)
        last = 0 if backward else chunk - 1
        src = d * GDN_V_HEADS + n_e * j + jnp.where(sel_lane < n_e, sel_lane, 2 * GDN_V_HEADS + sel_lane - n_e)
        sel = ((sel_row == src) & (sel_lane < 2 * n_e)).astype(BF16)
        picked_all = _dot_split(g_ref[...], sel)
        for c in range(n_chunks):
            rows = slice(c * chunk, (c + 1) * chunk)
            q = q_ref[rows, :]
            k = k_ref[rows, :]
            picked = picked_all[rows]
            cum = _split_dot(tri_c, picked)
            rep = lambda t, i: jnp.broadcast_to(t[:, i:i + 1], (chunk, LANES))
            g_rep = [rep(picked, e) for e in range(n_e)]
            b_rep = [rep(picked, n_e + e) for e in range(n_e)]
            cum_rep = [rep(cum, e) for e in range(n_e)]
            g_pair = jnp.where(left, g_rep[0], g_rep[1])
            cum_col = jnp.where(left, cum_rep[0], cum_rep[1])
            cum_row = jnp.sum(g_pair * tri_r, axis=0, keepdims=True)
            decay = jnp.where(incl, jnp.exp(jnp.where(incl, cum_col - cum_row, 0.0)), 0.0)
            k2 = jnp.concatenate([k, k], axis=0)
            kk = _dot_nt(k, k2)
            qk = _dot_nt(q, k2)
            a_list.append(jnp.where(strict, kk * decay, 0.0) * jnp.where(left, b_rep[0], b_rep[1]))
            qkd_list.append(qk * decay)
            k_list.append(k)
            rhs, tail = [], []
            for e in range(n_e):
                v = v_ref[rows, e * HEAD_DIM:(e + 1) * HEAD_DIM]
                g_last = cum_rep[e][last:last + 1, :]
                eg = jnp.exp(cum_rep[e])
                rhs.append(jnp.concatenate([v * b_rep[e], k * (b_rep[e] * eg)], axis=1))
                tail.append(jnp.exp(g_last - cum_rep[e]))
                gamma_list.append(jnp.exp(g_last))
                qdec_list.append(q * eg)
            rhs_list.append(jnp.concatenate(rhs, axis=0))
            tail_list.append(jnp.concatenate(tail, axis=0))

    lane_block = lane // chunk
    t_list = _unit_tri_inverse_minus_eye(a_list, same_block, lane_block)
    uw_list = [r + jnp.dot(_lane_block_diag(t, lane_block), r.astype(BF16), preferred_element_type=F32)
               for t, r in zip(t_list, rhs_list)]
    qo_list = [jnp.dot(_lane_block_diag(x, lane_block), y.astype(BF16), preferred_element_type=F32)
               for x, y in zip(qkd_list, uw_list)]
    for dc in range(2 * n_chunks):
        uw, tail = uw_list[dc], tail_list[dc]
        scaled = jnp.concatenate([uw[e * chunk:(e + 1) * chunk, h * HEAD_DIM:(h + 1) * HEAD_DIM]
                                  * tail[e * chunk:(e + 1) * chunk] for e in range(n_e) for h in range(2)],
                                 axis=1)
        kt = _dot_tn(k_list[dc], scaled)
        for e in range(n_e):
            b = n_e * dc + e
            base = 2 * HEAD_DIM * e
            qo = qo_list[dc][e * chunk:(e + 1) * chunk]
            psi_ref[b] = kt[:, base:base + HEAD_DIM]
            lhs_ref[b, :HEAD_DIM, :] = (-kt[:, base + HEAD_DIM:base + 2 * HEAD_DIM]).astype(BF16)
            lhs_ref[b, HEAD_DIM:, :] = (qdec_list[b] - qo[:, HEAD_DIM:]).astype(BF16)
            oin_ref[b] = qo[:, :HEAD_DIM]

    states = [st_ref[i] for i in range(2 * n_e)]
    for step in range(n_chunks):
        for d in range(2):
            c = n_chunks - 1 - step if d == 1 else step
            o_ref = dirs[d][4]
            for e in range(n_e):
                b = n_e * (d * n_chunks + c) + e
                s = states[n_e * d + e]
                res = jnp.dot(lhs_ref[b], s.astype(BF16), preferred_element_type=F32)
                states[n_e * d + e] = gamma_list[b] * s + res[:HEAD_DIM] + psi_ref[b]
                o_ref[c * chunk:(c + 1) * chunk, e * HEAD_DIM:(e + 1) * HEAD_DIM] = res[HEAD_DIM:] + oin_ref[b]
    for i in range(2 * n_e):
        st_ref[i] = states[i]


def _gated_delta(qkv, gates, n_chunks=8):
    seq = qkv.shape[0]
    blk = n_chunks * GDN_CHUNK
    nb = seq // blk
    rep = GDN_V_HEADS // GDN_K_HEADS
    vw = rep * HEAD_DIM
    v_off = 2 * GDN_K_HEADS * HEAD_DIM // vw
    n_chains = 2 * n_chunks * rep

    def specs(rowmap):
        return [pl.BlockSpec((blk, HEAD_DIM), lambda j, n: (rowmap(n), j)),
                pl.BlockSpec((blk, HEAD_DIM), lambda j, n: (rowmap(n), GDN_K_HEADS + j)),
                pl.BlockSpec((blk, vw), lambda j, n: (rowmap(n), v_off + j)),
                pl.BlockSpec((blk, LANES), lambda j, n: (rowmap(n), 0))]

    fwd = lambda n: n
    bwd = lambda n: nb - 1 - n
    out_sd = jax.ShapeDtypeStruct((seq, GDN_V_HEADS * HEAD_DIM), F32)
    return pl.pallas_call(
        functools.partial(_gdn_kernel, n_chunks=n_chunks, chunk=GDN_CHUNK),
        grid=(GDN_K_HEADS, nb),
        in_specs=specs(fwd) + specs(bwd),
        out_specs=[pl.BlockSpec((blk, vw), lambda j, n: (fwd(n), j)),
                   pl.BlockSpec((blk, vw), lambda j, n: (bwd(n), j))],
        out_shape=[out_sd, out_sd],
        scratch_shapes=[pltpu.VMEM((2 * rep, HEAD_DIM, HEAD_DIM), F32),
                        pltpu.VMEM((n_chains, HEAD_DIM + GDN_CHUNK, HEAD_DIM), BF16),
                        pltpu.VMEM((n_chains, HEAD_DIM, HEAD_DIM), F32),
                        pltpu.VMEM((n_chains, GDN_CHUNK, HEAD_DIM), F32)],
        compiler_params=_params(("parallel", "arbitrary")),
        name="gdn_delta_rule",
    )(qkv, qkv, qkv, gates, qkv, qkv, qkv, gates)


ATT_TQ = 256
ATT_OFFSETS = 3


def _att_kernel(q_ref, k_ref, v_ref, z_ref, o_ref, acc_ref, m_ref, l_ref, band_ref, *, seq):
    rel = (lax.broadcasted_iota(jnp.int32, band_ref.shape[1:], 0)
           - lax.broadcasted_iota(jnp.int32, band_ref.shape[1:], 1))
    for i in range(ATT_OFFSETS):
        band_ref[i] = (jnp.abs(rel + i * ATT_RADIUS) <= ATT_RADIUS).astype(F32)

    for p, dil in enumerate(DILATIONS):
        length = seq // dil
        tq = min(ATT_TQ, length)
        win = min(tq + 2 * ATT_RADIUS, length)
        tiles = length // tq
        first, final = p == 0, p == len(DILATIONS) - 1
        n_it = dil * tiles
        per_trip = max(t for t in (4, 2, 1) if n_it % t == 0)

        def scores(it, dil=dil, tq=tq, win=win, tiles=tiles, length=length):
            r = it // tiles
            m0 = (it % tiles) * tq
            k0 = jnp.clip(m0 - ATT_RADIUS, 0, length - win)
            if dil == 1:
                qsl = pl.ds(pl.multiple_of(m0, tq), tq)
                ksl = pl.ds(pl.multiple_of(k0, ATT_RADIUS), win)
            else:
                qsl = pl.ds(r + dil * m0, tq, stride=dil)
                ksl = pl.ds(r + dil * k0, win, stride=dil)
            s = _dot_nt(q_ref[qsl, :], k_ref[ksl, :])
            if win == tq + 2 * ATT_RADIUS:
                valid = band_ref[(m0 - k0) // ATT_RADIUS, :tq, :win] > 0.0
            else:
                qpos = m0 + lax.broadcasted_iota(jnp.int32, (tq, win), 0)
                kpos = k0 + lax.broadcasted_iota(jnp.int32, (tq, win), 1)
                valid = jnp.abs(qpos - kpos) <= ATT_RADIUS
            return qsl, ksl, jnp.where(valid, s, NEG_INF)

        def tile(qsl, ksl, s, tq=tq, win=win, first=first, final=final):
            slabs = [s[:, c:min(c + LANES, win)] for c in range(0, win, LANES)]
            m_new = jnp.broadcast_to(jnp.max(functools.reduce(jnp.maximum, slabs), axis=-1, keepdims=True),
                                     (tq, LANES))
            if not first:
                m_old = m_ref[qsl, :]
                m_new = jnp.maximum(m_old, m_new)
            e_slabs = [jnp.exp(sl - m_new[:, :sl.shape[1]]) for sl in slabs]
            l_new = jnp.broadcast_to(jnp.sum(functools.reduce(jnp.add, e_slabs), axis=-1, keepdims=True),
                                     (tq, LANES))
            acc = _dot(jnp.concatenate(e_slabs, axis=1), v_ref[ksl, :])
            if not first:
                corr = jnp.exp(m_old - m_new)
                l_new = l_new + corr * l_ref[qsl, :]
                acc = acc + corr * acc_ref[qsl, :]
            if final:
                return qsl, (acc / l_new * _silu(z_ref[qsl, :])).astype(o_ref.dtype), None, None
            return qsl, acc, m_new, l_new

        def body(trip, carry, per_trip=per_trip, final=final, scores=scores, tile=tile):
            staged = [scores(trip * per_trip + t) for t in range(per_trip)]
            done = [tile(*st) for st in staged]
            for qsl, main, m_new, l_new in done:
                if final:
                    o_ref[qsl, :] = main
                else:
                    acc_ref[qsl, :] = main
                    m_ref[qsl, :] = m_new
                    l_ref[qsl, :] = l_new
            return carry

        lax.fori_loop(0, n_it // per_trip, body, 0)


def _dilated_attention(qk, v, z):
    seq, width = v.shape
    heads = width // HEAD_DIM
    col = pl.BlockSpec((seq, HEAD_DIM), lambda h: (0, h))
    return pl.pallas_call(
        functools.partial(_att_kernel, seq=seq),
        grid=(heads,),
        in_specs=[col, pl.BlockSpec((seq, HEAD_DIM), lambda h: (0, heads + h)), col, col],
        out_specs=col,
        out_shape=jax.ShapeDtypeStruct((seq, width), BF16),
        scratch_shapes=[pltpu.VMEM((seq, HEAD_DIM), F32), pltpu.VMEM((seq, LANES), F32),
                        pltpu.VMEM((seq, LANES), F32),
                        pltpu.VMEM((ATT_OFFSETS, ATT_TQ, ATT_TQ + 2 * ATT_RADIUS), F32)],
        compiler_params=_params(("parallel",)),
        name="dilated_attention",
    )(qk, qk, v, z)


def _gdn_out_kernel(of_ref, ob_ref, z_ref, w_ref, y_ref):
    w = w_ref[...]
    for h in range(of_ref.shape[1] // HEAD_DIM):
        sl = slice(h * HEAD_DIM, (h + 1) * HEAD_DIM)
        o = of_ref[:, sl] + ob_ref[:, sl]
        inv = lax.rsqrt(jnp.mean(o * o, axis=-1, keepdims=True) + RMS_EPS)
        y_ref[:, sl] = (o * inv * w * z_ref[:, sl]).astype(y_ref.dtype)


def _gdn_out(o_f, o_b, z_silu, norm_w):
    seq, width = o_f.shape
    tm = min(512, seq)
    blk = pl.BlockSpec((tm, width), lambda i: (i, 0))
    return pl.pallas_call(
        _gdn_out_kernel,
        grid=(seq // tm,),
        in_specs=[blk, blk, blk, pl.BlockSpec((1, HEAD_DIM), lambda i: (0, 0))],
        out_specs=blk,
        out_shape=jax.ShapeDtypeStruct((seq, width), BF16),
        compiler_params=_params(("parallel",)),
        name="gdn_norm_gate",
    )(o_f, o_b, z_silu, norm_w)


def _layer_norm_rows(pre_ref, g_ref, b_ref, out_refs, n_col, tn):
    width = n_col * tn
    total = jnp.zeros((pre_ref.shape[1], 1), F32)
    for c in range(n_col):
        total = total + jnp.sum(pre_ref[c], axis=-1, keepdims=True)
    mu = total / width
    sq = jnp.zeros_like(total)
    for c in range(n_col):
        dev = pre_ref[c] - mu
        sq = sq + jnp.sum(dev * dev, axis=-1, keepdims=True)
    inv = lax.rsqrt(sq / width + LN_EPS)
    for c in range(n_col):
        sl = slice(c * tn, (c + 1) * tn)
        y = (pre_ref[c] - mu) * inv * g_ref[:, sl] + b_ref[:, sl]
        for ref in out_refs:
            ref[:, sl] = y.astype(ref.dtype)


def _outproj_kernel(ya_ref, yb_ref, wa_ref, wb_ref, x_ref, g_ref, b_ref, h_ref, pre_ref, *, n_col, tn, alpha):
    n = pl.program_id(1)
    mix = (jnp.dot(ya_ref[...], wa_ref[...], preferred_element_type=F32)
           + jnp.dot(yb_ref[...], wb_ref[...], preferred_element_type=F32))
    pre_ref[n] = alpha * x_ref[...] + mix

    @pl.when(n == n_col - 1)
    def _():
        _layer_norm_rows(pre_ref, g_ref, b_ref, (h_ref,), n_col, tn)


def _out_projection(y_a, y_b, w, x, gamma, beta, alpha, tm=512, tn=512):
    seq, d_model = x.shape
    tm = min(tm, seq)
    n_col = d_model // tn
    ka, kb = y_a.shape[1], y_b.shape[1]
    assert ka == kb and w.shape[0] == ka + kb
    row = lambda i, n: (i, 0)
    full = pl.BlockSpec((tm, d_model), row)
    vec = pl.BlockSpec((1, d_model), lambda i, n: (0, 0))
    return pl.pallas_call(
        functools.partial(_outproj_kernel, n_col=n_col, tn=tn, alpha=alpha),
        grid=(seq // tm, n_col),
        in_specs=[pl.BlockSpec((tm, ka), row), pl.BlockSpec((tm, kb), row),
                  pl.BlockSpec((ka, tn), lambda i, n: (0, n)), pl.BlockSpec((kb, tn), lambda i, n: (1, n)),
                  pl.BlockSpec((tm, tn), lambda i, n: (i, n)), vec, vec],
        out_specs=full,
        out_shape=jax.ShapeDtypeStruct((seq, d_model), F32),
        scratch_shapes=[pltpu.VMEM((n_col, tm, tn), F32)],
        compiler_params=_params(("parallel", "arbitrary")),
        name="out_proj_ln",
    )(y_a, y_b, w, w, x, gamma, beta)


def _ple_kernel(hb_ref, wg_ref, p_ref, wu_ref, h_ref, g_ref, b_ref, o_ref, pre_ref, *, n_col, tn, alpha):
    n = pl.program_id(1)
    gate = _sigmoid(jnp.dot(hb_ref[...], wg_ref[...], preferred_element_type=F32))
    up = jnp.dot(p_ref[...], wu_ref[...], preferred_element_type=F32)
    pre_ref[n] = alpha * h_ref[...] + gate * up

    @pl.when(n == n_col - 1)
    def _():
        _layer_norm_rows(pre_ref, g_ref, b_ref, (o_ref,), n_col, tn)


def _ple_layer(h_bf, w_gate, p_bf, w_up, h, gamma, beta, alpha, tm=512, tn=512):
    seq, d_model = h.shape
    tm = min(tm, seq)
    n_col = d_model // tn
    pd = p_bf.shape[1]
    row = lambda i, n: (i, 0)
    vec = pl.BlockSpec((1, d_model), lambda i, n: (0, 0))
    return pl.pallas_call(
        functools.partial(_ple_kernel, n_col=n_col, tn=tn, alpha=alpha),
        grid=(seq // tm, n_col),
        in_specs=[pl.BlockSpec((tm, d_model), row), pl.BlockSpec((d_model, tn), lambda i, n: (0, n)),
                  pl.BlockSpec((tm, pd), row), pl.BlockSpec((pd, tn), lambda i, n: (0, n)),
                  pl.BlockSpec((tm, tn), lambda i, n: (i, n)), vec, vec],
        out_specs=pl.BlockSpec((tm, d_model), row),
        out_shape=jax.ShapeDtypeStruct((seq, d_model), F32),
        scratch_shapes=[pltpu.VMEM((n_col, tm, tn), F32)],
        compiler_params=_params(("parallel", "arbitrary")),
        name="ple_ln",
    )(h_bf, w_gate, p_bf, w_up, h, gamma, beta)


def _rotary_tables(seq):
    half = HEAD_DIM // 2
    inv_freq = 1.0 / (jnp.float32(ROPE_THETA) ** (jnp.arange(half, dtype=F32) / half))
    ang = jnp.arange(seq).astype(F32)[:, None] * inv_freq[None, :]
    cos, sin = jnp.cos(ang), jnp.sin(ang)
    return jnp.concatenate([cos, cos], axis=-1), jnp.concatenate([-sin, sin], axis=-1)


def _layer(h, p, w_in, conv_w, a_log, dt_bias, gdn_norm_w, w_out, ln1_g, ln1_b, w_ple_gate, w_ple_up,
           ln2_g, ln2_b, alpha):
    seq, d_model = h.shape
    key_dim = GDN_K_HEADS * HEAD_DIM
    d_gdn = GDN_V_HEADS * HEAD_DIM
    d_att = ATT_HEADS * HEAD_DIM
    n_gate = 2 * GDN_V_HEADS
    o_qkv = 0
    o_za = 2 * key_dim + d_gdn
    o_ab = o_za + d_gdn
    o_qk = o_ab + 2 * n_gate
    o_vb = o_qk + 2 * d_att
    o_zb = o_vb + d_att

    x_bf = h.astype(BF16)
    w_t = jnp.swapaxes(w_in, 0, 1)
    w_bf = w_t.astype(BF16)

    qkv_a = _project(x_bf, w_bf, o_qkv, o_za - o_qkv, "proj_gdn_qkv")
    za = _project(x_bf, w_bf, o_za, d_gdn, "proj_gdn_gate", _epi_silu)
    w_ab = jnp.pad(w_bf[o_ab:o_qk], ((0, LANES - 2 * n_gate), (0, 0)))
    pad_vec = lambda t: jnp.pad(t.reshape(1, n_gate).astype(F32), ((0, 0), (0, LANES - n_gate)))
    vec_spec = pl.BlockSpec((1, LANES), lambda j, i: (0, 0))
    gates = _project(x_bf, w_ab, 0, LANES, "proj_gdn_decay", _epi_gates,
                     extra=(pad_vec(a_log), pad_vec(dt_bias)), extra_specs=(vec_spec, vec_spec))
    cos, sin = _rotary_tables(seq)
    tm_rot = min(1024, seq)
    tab_spec = pl.BlockSpec((tm_rot, HEAD_DIM), lambda j, i: (i, 0))
    tn_rot = 1024
    qk_b = _project(x_bf, w_bf, o_qk, 2 * d_att, "proj_att_qk",
                    functools.partial(_epi_rotary, n_q_blocks=d_att // tn_rot),
                    extra=(cos, sin), extra_specs=(tab_spec, tab_spec), tm=tm_rot, tn=tn_rot)
    v_b = _project(x_bf, w_bf, o_vb, d_att, "proj_att_v")
    z_b = _project(x_bf, w_bf, o_zb, d_att, "proj_att_gate")

    conv_w8 = jnp.pad(conv_w.astype(F32), ((0, SUBLANES - CONV_WIDTH), (0, 0)))
    qkv_n = _conv_norm(qkv_a, conv_w8)

    o_f, o_b = _gated_delta(qkv_n, gates)

    y_gdn = _gdn_out(o_f, o_b, za, gdn_norm_w.reshape(1, HEAD_DIM).astype(F32))

    y_att = _dilated_attention(qk_b, v_b, z_b)

    h1 = _out_projection(y_gdn, y_att, w_out.astype(BF16), h,
                         ln1_g.reshape(1, d_model), ln1_b.reshape(1, d_model), alpha)

    return _ple_layer(h1.astype(BF16), w_ple_gate.astype(BF16), p.astype(BF16), w_ple_up.astype(BF16), h1,
                      ln2_g.reshape(1, d_model), ln2_b.reshape(1, d_model), alpha)


def kernel(x, p, w_in, conv_w, a_log, dt_bias, gdn_norm_w, w_out, ln1_g, ln1_b, w_ple_gate, w_ple_up, ln2_g, ln2_b):
    batch = x.shape[0]
    depth = w_in.shape[0]
    alpha = (2 * depth) ** 0.25
    outs = []
    for b in range(batch):
        h = x[b]
        for i in range(depth):
            h = _layer(h, p[i, b], w_in[i], conv_w[i], a_log[i], dt_bias[i], gdn_norm_w[i], w_out[i],
                       ln1_g[i], ln1_b[i], w_ple_gate[i], w_ple_up[i], ln2_g[i], ln2_b[i], alpha)
        outs.append(h)
    return jnp.stack(outs)
```

```python
import functools
import itertools

import jax
import jax.numpy as jnp
from jax import lax
from jax.experimental import pallas as pl
from jax.experimental.pallas import tpu as pltpu

HEAD_DIM = 128
GDN_V_HEADS = 16
GDN_K_HEADS = 8
ATT_HEADS = 16
CONV_WIDTH = 5
GDN_CHUNK = 64
DILATIONS = (16, 4, 1)
ATT_RADIUS = 64
ROPE_THETA = 10000.0
LN_EPS = 1e-5
RMS_EPS = 1e-6
NEG_INF = -1e30
Q_SCALE = HEAD_DIM ** -0.5

LANES = 128
SUBLANES = 8
VMEM_LIMIT = 56 * 1024 * 1024

F32 = jnp.float32
BF16 = jnp.bfloat16
HI = lax.Precision.HIGHEST


def _params(sem, vmem=VMEM_LIMIT):
    return pltpu.CompilerParams(dimension_semantics=sem, vmem_limit_bytes=vmem)


def _sigmoid(x):
    return 1.0 / (1.0 + jnp.exp(-x))


def _silu(x):
    return x * _sigmoid(x)


def _dot(a, b):
    return jnp.dot(a.astype(BF16), b.astype(BF16), preferred_element_type=F32)


def _dot_nt(a, b):
    return lax.dot_general(a.astype(BF16), b.astype(BF16), (((1,), (1,)), ((), ())),
                           preferred_element_type=F32)


def _dot_tn(a, b):
    return lax.dot_general(a.astype(BF16), b.astype(BF16), (((0,), (0,)), ((), ())),
                           preferred_element_type=F32)


def _dot_hi(a, b):
    return jnp.dot(a, b, preferred_element_type=F32, precision=HI)


def _epi_none(acc, j):
    return acc


def _epi_silu(acc, j):
    return _silu(acc)


def _epi_rotary(acc, j, cos_ref, sin_ref, *, n_q_blocks):
    cos = cos_ref[...]
    sin = sin_ref[...]
    scale = jnp.where(j < n_q_blocks, Q_SCALE, 1.0).astype(F32)
    outs = []
    for c in range(acc.shape[1] // HEAD_DIM):
        t = acc[:, c * HEAD_DIM:(c + 1) * HEAD_DIM]
        outs.append((t * cos + pltpu.roll(t, HEAD_DIM // 2, axis=1) * sin) * scale)
    return jnp.concatenate(outs, axis=1)


def _epi_gates(acc, j, alog_ref, dtb_ref):
    t = acc + dtb_ref[...]
    softplus = jnp.maximum(t, 0.0) + jnp.log1p(jnp.exp(-jnp.abs(t)))
    g = -jnp.exp(alog_ref[...]) * softplus
    beta = _sigmoid(acc)
    lane = lax.broadcasted_iota(jnp.int32, acc.shape, 1)
    n_gate = 2 * GDN_V_HEADS
    return jnp.where(lane < n_gate, g, jnp.where(lane < 2 * n_gate, beta, 0.0))


def _mm_kernel(x_ref, w_ref, *refs, epilogue):
    o_ref = refs[-1]
    acc = lax.dot_general(x_ref[...], w_ref[...], (((1,), (1,)), ((), ())), preferred_element_type=F32)
    o_ref[...] = epilogue(acc, pl.program_id(0), *refs[:-1]).astype(o_ref.dtype)


def _project(x, w_t, row0, n, name, epilogue=_epi_none, extra=(), extra_specs=(), out_dtype=F32, tm=1024, tn=1024):
    m, kd = x.shape
    tm, tn = min(tm, m), min(tn, n)
    align = 2 * SUBLANES
    assert m % tm == 0 and n % tn == 0 and row0 + n <= w_t.shape[0] and row0 % align == 0 and tn % align == 0
    return pl.pallas_call(
        functools.partial(_mm_kernel, epilogue=epilogue),
        grid=(n // tn, m // tm),
        in_specs=[pl.BlockSpec((tm, kd), lambda j, i: (i, 0)),
                  pl.BlockSpec((pl.Element(tn), pl.Element(kd)),
                               lambda j, i: (pl.multiple_of(row0 + j * tn, align), 0)), *extra_specs],
        out_specs=pl.BlockSpec((tm, tn), lambda j, i: (i, j)),
        out_shape=jax.ShapeDtypeStruct((m, n), out_dtype),
        compiler_params=_params(("parallel", "parallel")),
        name=name,
    )(x, w_t, *extra)


def _conv_kernel(u_ref, w_ref, o_ref, *, seq, tile, n_q, n_k):
    j = pl.program_id(0)
    n_tiles = seq // tile
    w = w_ref[...]
    pad = (CONV_WIDTH - 1) // 2
    ext_rows = tile + 2 * SUBLANES

    def finish(t0, acc):
        y = _silu(acc)
        inv = lax.rsqrt(jnp.sum(y * y, axis=-1, keepdims=True) + RMS_EPS)
        scale = jnp.where(j < n_q, inv * Q_SCALE, jnp.where(j < n_q + n_k, inv, 1.0))
        o_ref[pl.ds(t0, tile), :] = y * scale

    def edge(i):
        t0 = i * tile
        zeros = jnp.zeros((SUBLANES, LANES), F32)
        prev = u_ref[t0 - SUBLANES:t0, :] if i > 0 else zeros
        nxt = u_ref[t0 + tile:t0 + tile + SUBLANES, :] if i < n_tiles - 1 else zeros
        ext = jnp.concatenate([prev, u_ref[t0:t0 + tile, :], nxt], axis=0)
        acc = jnp.zeros((tile, LANES), F32)
        for tap in range(CONV_WIDTH):
            shift = (pad - tap) % ext_rows
            sh = ext if shift == 0 else pltpu.roll(ext, shift, axis=0)
            acc = acc + sh[SUBLANES:SUBLANES + tile] * w[tap:tap + 1, :]
        finish(t0, acc)

    def body(i, carry):
        t0 = pl.multiple_of(i * tile, tile)
        acc = jnp.zeros((tile, LANES), F32)
        for tap in range(CONV_WIDTH):
            acc = acc + u_ref[pl.ds(t0 + (tap - pad), tile), :] * w[tap:tap + 1, :]
        finish(t0, acc)
        return carry

    edge(0)
    if n_tiles > 1:
        lax.fori_loop(1, n_tiles - 1, body, 0)
        edge(n_tiles - 1)


def _conv_norm(u, conv_w8):
    seq, chans = u.shape
    tile = min(256, seq)
    return pl.pallas_call(
        functools.partial(_conv_kernel, seq=seq, tile=tile, n_q=GDN_K_HEADS, n_k=GDN_K_HEADS),
        grid=(chans // LANES,),
        in_specs=[pl.BlockSpec((seq, LANES), lambda j: (0, j)),
                  pl.BlockSpec((SUBLANES, LANES), lambda j: (0, j))],
        out_specs=pl.BlockSpec((seq, LANES), lambda j: (0, j)),
        out_shape=jax.ShapeDtypeStruct((seq, chans), F32),
        compiler_params=_params(("parallel",)),
        name="gdn_conv_norm",
    )(u, conv_w8)


GDN_SUB = 16


def _lane_block_diag(x, lane_block):
    xb = x.astype(BF16)
    zero = jnp.zeros_like(xb)
    n_blocks = x.shape[1] // x.shape[0]
    return jnp.concatenate([jnp.where(lane_block == r, xb, zero) for r in range(n_blocks)], axis=0)


def _packed_dot(x, y, lane_block):
    return jnp.dot(x.astype(BF16), _lane_block_diag(y, lane_block), preferred_element_type=F32)


def _unit_tri_inverse_minus_eye(a_list, same_block, lane_block):
    dot = functools.partial(_packed_dot, lane_block=lane_block)
    d = [jnp.where(same_block, a, 0.0) for a in a_list]
    off = [a - x for a, x in zip(a_list, d)]
    p = [-x for x in d]
    e = d
    for _ in range(3):
        e = [dot(x, x) for x in e]
        p = [x + y + dot(x, y) for x, y in zip(p, e)]
    m = [y + dot(x, y) for x, y in zip(p, off)]
    m2 = [dot(x, x) for x in m]
    q = [y - x - dot(x, y) for x, y in zip(m, m2)]
    return [x + y + dot(x, y) for x, y in zip(q, p)]


def _dot_split(a, b_bf):
    hi = a.astype(BF16)
    lo = (a - hi.astype(F32)).astype(BF16)
    return jnp.dot(hi, b_bf, preferred_element_type=F32) + jnp.dot(lo, b_bf, preferred_element_type=F32)


def _split_dot(a_bf, b):
    hi = b.astype(BF16)
    lo = (b - hi.astype(F32)).astype(BF16)
    return jnp.dot(a_bf, hi, preferred_element_type=F32) + jnp.dot(a_bf, lo, preferred_element_type=F32)


def _gdn_kernel(qf_ref, kf_ref, vf_ref, gf_ref, qb_ref, kb_ref, vb_ref, gb_ref,
                of_ref, ob_ref, st_ref, lhs_ref, psi_ref, oin_ref, *, n_chunks, chunk, n_kh):
    j = pl.program_id(0)

    @pl.when(pl.program_id(1) == 0)
    def _():
        st_ref[...] = jnp.zeros_like(st_ref)

    n_e = 2
    width = n_e * chunk
    row = lax.broadcasted_iota(jnp.int32, (chunk, width), 0)
    lane = lax.broadcasted_iota(jnp.int32, (chunk, width), 1)
    col = lane % chunk
    left = lane < chunk
    same_block = (row // GDN_SUB) == (col // GDN_SUB)
    r2 = lax.broadcasted_iota(jnp.int32, (chunk, chunk), 0)
    c2 = lax.broadcasted_iota(jnp.int32, (chunk, chunk), 1)
    sel_row = lax.broadcasted_iota(jnp.int32, (LANES, LANES), 0)
    sel_lane = lax.broadcasted_iota(jnp.int32, (LANES, LANES), 1)
    dirs = ((qf_ref, kf_ref, vf_ref, gf_ref, of_ref), (qb_ref, kb_ref, vb_ref, gb_ref, ob_ref))

    a_list, qkd_list, rhs_list, tail_list, k_list, gamma_list, qdec_list = [], [], [], [], [], [], []
    for kh, (d, (q_ref, k_ref, v_ref, g_ref, _)) in itertools.product(range(n_kh), enumerate(dirs)):
        backward = d == 1
        incl = (col >= row) if backward else (col <= row)
        strict = (col > row) if backward else (col < row)
        tri_c = ((c2 >= r2) if backward else (c2 <= r2)).astype(BF16)
        tri_r = ((row >= col) if backward else (row <= col)).astype(F32)
        last = 0 if backward else chunk - 1
        head0 = n_e * (n_kh * j + kh)
        src = d * GDN_V_HEADS + head0 + jnp.where(sel_lane < n_e, sel_lane, 2 * GDN_V_HEADS + sel_lane - n_e)
        sel = ((sel_row == src) & (sel_lane < 2 * n_e)).astype(BF16)
        picked_all = _dot_split(g_ref[...], sel)
        for c in range(n_chunks):
            rows = slice(c * chunk, (c + 1) * chunk)
            q = q_ref[rows, kh * HEAD_DIM:(kh + 1) * HEAD_DIM]
            k = k_ref[rows, kh * HEAD_DIM:(kh + 1) * HEAD_DIM]
            picked = picked_all[rows]
            cum = _split_dot(tri_c, picked)
            rep = lambda t, i: jnp.broadcast_to(t[:, i:i + 1], (chunk, LANES))
            g_rep = [rep(picked, e) for e in range(n_e)]
            b_rep = [rep(picked, n_e + e) for e in range(n_e)]
            cum_rep = [rep(cum, e) for e in range(n_e)]
            g_pair = jnp.where(left, g_rep[0], g_rep[1])
            cum_col = jnp.where(left, cum_rep[0], cum_rep[1])
            cum_row = jnp.sum(g_pair * tri_r, axis=0, keepdims=True)
            decay = jnp.where(incl, jnp.exp(jnp.where(incl, cum_col - cum_row, 0.0)), 0.0)
            k2 = jnp.concatenate([k, k], axis=0)
            kk = _dot_nt(k, k2)
            qk = _dot_nt(q, k2)
            a_list.append(jnp.where(strict, kk * decay, 0.0) * jnp.where(left, b_rep[0], b_rep[1]))
            qkd_list.append(qk * decay)
            k_list.append(k)
            rhs, tail = [], []
            for e in range(n_e):
                v = v_ref[rows, (kh * n_e + e) * HEAD_DIM:(kh * n_e + e + 1) * HEAD_DIM]
                g_last = cum_rep[e][last:last + 1, :]
                eg = jnp.exp(cum_rep[e])
                rhs.append(jnp.concatenate([v * b_rep[e], k * (b_rep[e] * eg)], axis=1))
                tail.append(jnp.exp(g_last - cum_rep[e]))
                gamma_list.append(jnp.exp(g_last))
                qdec_list.append(q * eg)
            rhs_list.append(jnp.concatenate(rhs, axis=0))
            tail_list.append(jnp.concatenate(tail, axis=0))

    lane_block = lane // chunk
    t_list = _unit_tri_inverse_minus_eye(a_list, same_block, lane_block)
    uw_list = [r + jnp.dot(_lane_block_diag(t, lane_block), r.astype(BF16), preferred_element_type=F32)
               for t, r in zip(t_list, rhs_list)]
    qo_list = [jnp.dot(_lane_block_diag(x, lane_block), y.astype(BF16), preferred_element_type=F32)
               for x, y in zip(qkd_list, uw_list)]
    for dc in range(n_kh * 2 * n_chunks):
        uw, tail = uw_list[dc], tail_list[dc]
        scaled = jnp.concatenate([uw[e * chunk:(e + 1) * chunk, h * HEAD_DIM:(h + 1) * HEAD_DIM]
                                  * tail[e * chunk:(e + 1) * chunk] for e in range(n_e) for h in range(2)],
                                 axis=1)
        kt = _dot_tn(k_list[dc], scaled)
        for e in range(n_e):
            b = n_e * dc + e
            base = 2 * HEAD_DIM * e
            qo = qo_list[dc][e * chunk:(e + 1) * chunk]
            psi_ref[b] = kt[:, base:base + HEAD_DIM]
            lhs_ref[b, :HEAD_DIM, :] = (-kt[:, base + HEAD_DIM:base + 2 * HEAD_DIM]).astype(BF16)
            lhs_ref[b, HEAD_DIM:, :] = (qdec_list[b] - qo[:, HEAD_DIM:]).astype(BF16)
            oin_ref[b] = qo[:, :HEAD_DIM]

    n_states = n_kh * 2 * n_e
    states = [st_ref[i] for i in range(n_states)]
    for step in range(n_chunks):
        for kh, d, e in itertools.product(range(n_kh), range(2), range(n_e)):
            c = n_chunks - 1 - step if d == 1 else step
            o_ref = dirs[d][4]
            b = n_e * ((kh * 2 + d) * n_chunks + c) + e
            i = (kh * 2 + d) * n_e + e
            s = states[i]
            res = jnp.dot(lhs_ref[b], s.astype(BF16), preferred_element_type=F32)
            states[i] = gamma_list[b] * s + res[:HEAD_DIM] + psi_ref[b]
            lanes = slice((kh * n_e + e) * HEAD_DIM, (kh * n_e + e + 1) * HEAD_DIM)
            o_ref[c * chunk:(c + 1) * chunk, lanes] = res[HEAD_DIM:] + oin_ref[b]
    for i in range(n_states):
        st_ref[i] = states[i]


def _gated_delta(qkv, gates, n_chunks=2, n_kh=4):
    seq = qkv.shape[0]
    blk = n_chunks * GDN_CHUNK
    nb = seq // blk
    rep = GDN_V_HEADS // GDN_K_HEADS
    kw = n_kh * HEAD_DIM
    vw = n_kh * rep * HEAD_DIM
    groups = GDN_K_HEADS // n_kh
    v_off = 2 * GDN_K_HEADS * HEAD_DIM // vw
    n_chains = n_kh * 2 * n_chunks * rep

    def specs(rowmap):
        return [pl.BlockSpec((blk, kw), lambda j, n: (rowmap(n), j)),
                pl.BlockSpec((blk, kw), lambda j, n: (rowmap(n), groups + j)),
                pl.BlockSpec((blk, vw), lambda j, n: (rowmap(n), v_off + j)),
                pl.BlockSpec((blk, LANES), lambda j, n: (rowmap(n), 0))]

    fwd = lambda n: n
    bwd = lambda n: nb - 1 - n
    out_sd = jax.ShapeDtypeStruct((seq, GDN_V_HEADS * HEAD_DIM), F32)
    return pl.pallas_call(
        functools.partial(_gdn_kernel, n_chunks=n_chunks, chunk=GDN_CHUNK, n_kh=n_kh),
        grid=(groups, nb),
        in_specs=specs(fwd) + specs(bwd),
        out_specs=[pl.BlockSpec((blk, vw), lambda j, n: (fwd(n), j)),
                   pl.BlockSpec((blk, vw), lambda j, n: (bwd(n), j))],
        out_shape=[out_sd, out_sd],
        scratch_shapes=[pltpu.VMEM((n_kh * 2 * rep, HEAD_DIM, HEAD_DIM), F32),
                        pltpu.VMEM((n_chains, HEAD_DIM + GDN_CHUNK, HEAD_DIM), BF16),
                        pltpu.VMEM((n_chains, HEAD_DIM, HEAD_DIM), F32),
                        pltpu.VMEM((n_chains, GDN_CHUNK, HEAD_DIM), F32)],
        compiler_params=_params(("parallel", "arbitrary")),
        name="gdn_delta_rule",
    )(qkv, qkv, qkv, gates, qkv, qkv, qkv, gates)


ATT_TQ = 256
ATT_TILING = {16: (256, 4), 4: (128, 8), 1: (128, 8)}
ATT_OFFSETS = 3


def _att_kernel(q_ref, k_ref, v_ref, z_ref, o_ref, acc_ref, m_ref, l_ref, band_ref, *, seq):
    rel = (lax.broadcasted_iota(jnp.int32, band_ref.shape[1:], 0)
           - lax.broadcasted_iota(jnp.int32, band_ref.shape[1:], 1))
    for i in range(ATT_OFFSETS):
        band_ref[i] = (jnp.abs(rel + i * ATT_RADIUS) <= ATT_RADIUS).astype(F32)

    for p, dil in enumerate(DILATIONS):
        length = seq // dil
        tq = min(ATT_TILING[dil][0], length)
        win = min(tq + 2 * ATT_RADIUS, length)
        tiles = length // tq
        first, final = p == 0, p == len(DILATIONS) - 1
        n_it = dil * tiles
        per_trip = max(t for t in (8, 4, 2, 1) if n_it % t == 0 and t <= ATT_TILING[dil][1])

        def scores(it, dil=dil, tq=tq, win=win, tiles=tiles, length=length):
            r = it // tiles
            m0 = (it % tiles) * tq
            k0 = jnp.clip(m0 - ATT_RADIUS, 0, length - win)
            if dil == 1:
                qsl = pl.ds(pl.multiple_of(m0, tq), tq)
                ksl = pl.ds(pl.multiple_of(k0, ATT_RADIUS), win)
            else:
                qsl = pl.ds(r + dil * m0, tq, stride=dil)
                ksl = pl.ds(r + dil * k0, win, stride=dil)
            s = _dot_nt(q_ref[qsl, :], k_ref[ksl, :])
            if win == tq + 2 * ATT_RADIUS:
                valid = band_ref[(m0 - k0) // ATT_RADIUS, :tq, :win] > 0.0
            else:
                qpos = m0 + lax.broadcasted_iota(jnp.int32, (tq, win), 0)
                kpos = k0 + lax.broadcasted_iota(jnp.int32, (tq, win), 1)
                valid = jnp.abs(qpos - kpos) <= ATT_RADIUS
            return qsl, ksl, jnp.where(valid, s, NEG_INF)

        def tile(qsl, ksl, s, tq=tq, win=win, first=first, final=final):
            slabs = [s[:, c:min(c + LANES, win)] for c in range(0, win, LANES)]
            m_new = jnp.broadcast_to(jnp.max(functools.reduce(jnp.maximum, slabs), axis=-1, keepdims=True),
                                     (tq, LANES))
            if not first:
                m_old = m_ref[qsl, :]
                m_new = jnp.maximum(m_old, m_new)
            e_slabs = [jnp.exp(sl - m_new[:, :sl.shape[1]]) for sl in slabs]
            l_new = jnp.broadcast_to(jnp.sum(functools.reduce(jnp.add, e_slabs), axis=-1, keepdims=True),
                                     (tq, LANES))
            acc = _dot(jnp.concatenate(e_slabs, axis=1), v_ref[ksl, :])
            if not first:
                corr = jnp.exp(m_old - m_new)
                l_new = l_new + corr * l_ref[qsl, :]
                acc = acc + corr * acc_ref[qsl, :]
            if final:
                return qsl, (acc / l_new * _silu(z_ref[qsl, :])).astype(o_ref.dtype), None, None
            return qsl, acc, m_new, l_new

        def body(trip, carry, per_trip=per_trip, final=final, scores=scores, tile=tile):
            staged = [scores(trip * per_trip + t) for t in range(per_trip)]
            done = [tile(*st) for st in staged]
            for qsl, main, m_new, l_new in done:
                if final:
                    o_ref[qsl, :] = main
                else:
                    acc_ref[qsl, :] = main
                    m_ref[qsl, :] = m_new
                    l_ref[qsl, :] = l_new
            return carry

        lax.fori_loop(0, n_it // per_trip, body, 0)


def _dilated_attention(qk, v, z):
    seq, width = v.shape
    heads = width // HEAD_DIM
    col = pl.BlockSpec((seq, HEAD_DIM), lambda h: (0, h))
    return pl.pallas_call(
        functools.partial(_att_kernel, seq=seq),
        grid=(heads,),
        in_specs=[col, pl.BlockSpec((seq, HEAD_DIM), lambda h: (0, heads + h)), col, col],
        out_specs=col,
        out_shape=jax.ShapeDtypeStruct((seq, width), BF16),
        scratch_shapes=[pltpu.VMEM((seq, HEAD_DIM), F32), pltpu.VMEM((seq, LANES), F32),
                        pltpu.VMEM((seq, LANES), F32),
                        pltpu.VMEM((ATT_OFFSETS, ATT_TQ, ATT_TQ + 2 * ATT_RADIUS), F32)],
        compiler_params=_params(("parallel",)),
        name="dilated_attention",
    )(qk, qk, v, z)


def _gdn_out_kernel(of_ref, ob_ref, z_ref, w_ref, y_ref):
    w = w_ref[...]
    for h in range(of_ref.shape[1] // HEAD_DIM):
        sl = slice(h * HEAD_DIM, (h + 1) * HEAD_DIM)
        o = of_ref[:, sl] + ob_ref[:, sl]
        inv = lax.rsqrt(jnp.mean(o * o, axis=-1, keepdims=True) + RMS_EPS)
        y_ref[:, sl] = (o * inv * w * z_ref[:, sl]).astype(y_ref.dtype)


def _gdn_out(o_f, o_b, z_silu, norm_w):
    seq, width = o_f.shape
    tm = min(512, seq)
    blk = pl.BlockSpec((tm, width), lambda i: (i, 0))
    return pl.pallas_call(
        _gdn_out_kernel,
        grid=(seq // tm,),
        in_specs=[blk, blk, blk, pl.BlockSpec((1, HEAD_DIM), lambda i: (0, 0))],
        out_specs=blk,
        out_shape=jax.ShapeDtypeStruct((seq, width), BF16),
        compiler_params=_params(("parallel",)),
        name="gdn_norm_gate",
    )(o_f, o_b, z_silu, norm_w)


def _layer_norm_rows(pre_ref, g_ref, b_ref, out_refs, n_col, tn):
    width = n_col * tn
    total = jnp.zeros((pre_ref.shape[1], 1), F32)
    for c in range(n_col):
        total = total + jnp.sum(pre_ref[c], axis=-1, keepdims=True)
    mu = total / width
    sq = jnp.zeros_like(total)
    for c in range(n_col):
        dev = pre_ref[c] - mu
        sq = sq + jnp.sum(dev * dev, axis=-1, keepdims=True)
    inv = lax.rsqrt(sq / width + LN_EPS)
    for c in range(n_col):
        sl = slice(c * tn, (c + 1) * tn)
        y = (pre_ref[c] - mu) * inv * g_ref[:, sl] + b_ref[:, sl]
        for ref in out_refs:
            ref[:, sl] = y.astype(ref.dtype)


def _outproj_kernel(ya_ref, yb_ref, wa_ref, wb_ref, x_ref, g_ref, b_ref, h_ref, pre_ref, *, n_col, tn, alpha):
    n = pl.program_id(1)
    mix = (jnp.dot(ya_ref[...], wa_ref[...], preferred_element_type=F32)
           + jnp.dot(yb_ref[...], wb_ref[...], preferred_element_type=F32))
    pre_ref[n] = alpha * x_ref[...] + mix

    @pl.when(n == n_col - 1)
    def _():
        _layer_norm_rows(pre_ref, g_ref, b_ref, (h_ref,), n_col, tn)


def _out_projection(y_a, y_b, w, x, gamma, beta, alpha, tm=512, tn=512):
    seq, d_model = x.shape
    tm = min(tm, seq)
    n_col = d_model // tn
    ka, kb = y_a.shape[1], y_b.shape[1]
    assert ka == kb and w.shape[0] == ka + kb
    row = lambda i, n: (i, 0)
    full = pl.BlockSpec((tm, d_model), row)
    vec = pl.BlockSpec((1, d_model), lambda i, n: (0, 0))
    return pl.pallas_call(
        functools.partial(_outproj_kernel, n_col=n_col, tn=tn, alpha=alpha),
        grid=(seq // tm, n_col),
        in_specs=[pl.BlockSpec((tm, ka), row), pl.BlockSpec((tm, kb), row),
                  pl.BlockSpec((ka, tn), lambda i, n: (0, n)), pl.BlockSpec((kb, tn), lambda i, n: (1, n)),
                  pl.BlockSpec((tm, tn), lambda i, n: (i, n)), vec, vec],
        out_specs=full,
        out_shape=jax.ShapeDtypeStruct((seq, d_model), F32),
        scratch_shapes=[pltpu.VMEM((n_col, tm, tn), F32)],
        compiler_params=_params(("parallel", "arbitrary")),
        name="out_proj_ln",
    )(y_a, y_b, w, w, x, gamma, beta)


def _ple_kernel(hb_ref, wg_ref, p_ref, wu_ref, h_ref, g_ref, b_ref, o_ref, pre_ref, *, n_col, tn, alpha):
    n = pl.program_id(1)
    gate = _sigmoid(jnp.dot(hb_ref[...], wg_ref[...], preferred_element_type=F32))
    up = jnp.dot(p_ref[...], wu_ref[...], preferred_element_type=F32)
    pre_ref[n] = alpha * h_ref[...] + gate * up

    @pl.when(n == n_col - 1)
    def _():
        _layer_norm_rows(pre_ref, g_ref, b_ref, (o_ref,), n_col, tn)


def _ple_layer(h_bf, w_gate, p_bf, w_up, h, gamma, beta, alpha, tm=512, tn=512):
    seq, d_model = h.shape
    tm = min(tm, seq)
    n_col = d_model // tn
    pd = p_bf.shape[1]
    row = lambda i, n: (i, 0)
    vec = pl.BlockSpec((1, d_model), lambda i, n: (0, 0))
    return pl.pallas_call(
        functools.partial(_ple_kernel, n_col=n_col, tn=tn, alpha=alpha),
        grid=(seq // tm, n_col),
        in_specs=[pl.BlockSpec((tm, d_model), row), pl.BlockSpec((d_model, tn), lambda i, n: (0, n)),
                  pl.BlockSpec((tm, pd), row), pl.BlockSpec((pd, tn), lambda i, n: (0, n)),
                  pl.BlockSpec((tm, tn), lambda i, n: (i, n)), vec, vec],
        out_specs=pl.BlockSpec((tm, d_model), row),
        out_shape=jax.ShapeDtypeStruct((seq, d_model), F32),
        scratch_shapes=[pltpu.VMEM((n_col, tm, tn), F32)],
        compiler_params=_params(("parallel", "arbitrary")),
        name="ple_ln",
    )(h_bf, w_gate, p_bf, w_up, h, gamma, beta)


def _rotary_tables(seq):
    half = HEAD_DIM // 2
    inv_freq = 1.0 / (jnp.float32(ROPE_THETA) ** (jnp.arange(half, dtype=F32) / half))
    ang = jnp.arange(seq).astype(F32)[:, None] * inv_freq[None, :]
    cos, sin = jnp.cos(ang), jnp.sin(ang)
    return jnp.concatenate([cos, cos], axis=-1), jnp.concatenate([-sin, sin], axis=-1)


def _layer(h, p, w_in, conv_w, a_log, dt_bias, gdn_norm_w, w_out, ln1_g, ln1_b, w_ple_gate, w_ple_up,
           ln2_g, ln2_b, alpha):
    seq, d_model = h.shape
    key_dim = GDN_K_HEADS * HEAD_DIM
    d_gdn = GDN_V_HEADS * HEAD_DIM
    d_att = ATT_HEADS * HEAD_DIM
    n_gate = 2 * GDN_V_HEADS
    o_qkv = 0
    o_za = 2 * key_dim + d_gdn
    o_ab = o_za + d_gdn
    o_qk = o_ab + 2 * n_gate
    o_vb = o_qk + 2 * d_att
    o_zb = o_vb + d_att

    x_bf = h.astype(BF16)
    w_t = jnp.swapaxes(w_in, 0, 1)
    w_bf = w_t.astype(BF16)

    qkv_a = _project(x_bf, w_bf, o_qkv, o_za - o_qkv, "proj_gdn_qkv")
    za = _project(x_bf, w_bf, o_za, d_gdn, "proj_gdn_gate", _epi_silu)
    w_ab = jnp.pad(w_bf[o_ab:o_qk], ((0, LANES - 2 * n_gate), (0, 0)))
    pad_vec = lambda t: jnp.pad(t.reshape(1, n_gate).astype(F32), ((0, 0), (0, LANES - n_gate)))
    vec_spec = pl.BlockSpec((1, LANES), lambda j, i: (0, 0))
    gates = _project(x_bf, w_ab, 0, LANES, "proj_gdn_decay", _epi_gates,
                     extra=(pad_vec(a_log), pad_vec(dt_bias)), extra_specs=(vec_spec, vec_spec))
    cos, sin = _rotary_tables(seq)
    tm_rot = min(1024, seq)
    tab_spec = pl.BlockSpec((tm_rot, HEAD_DIM), lambda j, i: (i, 0))
    tn_rot = 1024
    qk_b = _project(x_bf, w_bf, o_qk, 2 * d_att, "proj_att_qk",
                    functools.partial(_epi_rotary, n_q_blocks=d_att // tn_rot),
                    extra=(cos, sin), extra_specs=(tab_spec, tab_spec), tm=tm_rot, tn=tn_rot)
    v_b = _project(x_bf, w_bf, o_vb, d_att, "proj_att_v")
    z_b = _project(x_bf, w_bf, o_zb, d_att, "proj_att_gate")

    conv_w8 = jnp.pad(conv_w.astype(F32), ((0, SUBLANES - CONV_WIDTH), (0, 0)))
    qkv_n = _conv_norm(qkv_a, conv_w8)

    o_f, o_b = _gated_delta(qkv_n, gates)

    y_gdn = _gdn_out(o_f, o_b, za, gdn_norm_w.reshape(1, HEAD_DIM).astype(F32))

    y_att = _dilated_attention(qk_b, v_b, z_b)

    h1 = _out_projection(y_gdn, y_att, w_out.astype(BF16), h,
                         ln1_g.reshape(1, d_model), ln1_b.reshape(1, d_model), alpha)

    return _ple_layer(h1.astype(BF16), w_ple_gate.astype(BF16), p.astype(BF16), w_ple_up.astype(BF16), h1,
                      ln2_g.reshape(1, d_model), ln2_b.reshape(1, d_model), alpha)


def kernel(x, p, w_in, conv_w, a_log, dt_bias, gdn_norm_w, w_out, ln1_g, ln1_b, w_ple_gate, w_ple_up, ln2_g, ln2_b):
    batch = x.shape[0]
    depth = w_in.shape[0]
    alpha = (2 * depth) ** 0.25
    outs = []
    for b in range(batch):
        h = x[b]
        for i in range(depth):
            h = _layer(h, p[i, b], w_in[i], conv_w[i], a_log[i], dt_bias[i], gdn_norm_w[i], w_out[i],
                       ln1_g[i], ln1_b[i], w_ple_gate[i], w_ple_up[i], ln2_g[i], ln2_b[i], alpha)
        outs.append(h)
    return jnp.stack(outs)
```

```python
import functools
import itertools

import jax
import jax.numpy as jnp
from jax import lax
from jax.experimental import pallas as pl
from jax.experimental.pallas import tpu as pltpu

HEAD_DIM = 128
GDN_V_HEADS = 16
GDN_K_HEADS = 8
ATT_HEADS = 16
CONV_WIDTH = 5
GDN_CHUNK = 64
DILATIONS = (16, 4, 1)
ATT_RADIUS = 64
ROPE_THETA = 10000.0
LN_EPS = 1e-5
RMS_EPS = 1e-6
NEG_INF = -1e30
Q_SCALE = HEAD_DIM ** -0.5

LANES = 128
SUBLANES = 8
VMEM_LIMIT = 56 * 1024 * 1024

F32 = jnp.float32
BF16 = jnp.bfloat16
HI = lax.Precision.HIGHEST


def _params(sem, vmem=VMEM_LIMIT):
    return pltpu.CompilerParams(dimension_semantics=sem, vmem_limit_bytes=vmem)


def _sigmoid(x):
    return 1.0 / (1.0 + jnp.exp(-x))


def _silu(x):
    return x * _sigmoid(x)


def _dot(a, b):
    return jnp.dot(a.astype(BF16), b.astype(BF16), preferred_element_type=F32)


def _dot_nt(a, b):
    return lax.dot_general(a.astype(BF16), b.astype(BF16), (((1,), (1,)), ((), ())),
                           preferred_element_type=F32)


def _dot_tn(a, b):
    return lax.dot_general(a.astype(BF16), b.astype(BF16), (((0,), (0,)), ((), ())),
                           preferred_element_type=F32)


def _dot_hi(a, b):
    return jnp.dot(a, b, preferred_element_type=F32, precision=HI)


def _epi_none(acc, j):
    return acc


def _epi_silu(acc, j):
    return _silu(acc)


def _epi_rotary(acc, j, cos_ref, sin_ref, *, n_q_blocks):
    cos = cos_ref[...]
    sin = sin_ref[...]
    scale = jnp.where(j < n_q_blocks, Q_SCALE, 1.0).astype(F32)
    outs = []
    for c in range(acc.shape[1] // HEAD_DIM):
        t = acc[:, c * HEAD_DIM:(c + 1) * HEAD_DIM]
        outs.append((t * cos + pltpu.roll(t, HEAD_DIM // 2, axis=1) * sin) * scale)
    return jnp.concatenate(outs, axis=1)


def _epi_gates(acc, j, alog_ref, dtb_ref):
    t = acc + dtb_ref[...]
    softplus = jnp.maximum(t, 0.0) + jnp.log1p(jnp.exp(-jnp.abs(t)))
    g = -jnp.exp(alog_ref[...]) * softplus
    beta = _sigmoid(acc)
    lane = lax.broadcasted_iota(jnp.int32, acc.shape, 1)
    n_gate = 2 * GDN_V_HEADS
    return jnp.where(lane < n_gate, g, jnp.where(lane < 2 * n_gate, beta, 0.0))


def _mm_kernel(x_ref, w_ref, *refs, epilogue):
    o_ref = refs[-1]
    acc = lax.dot_general(x_ref[...], w_ref[...], (((1,), (1,)), ((), ())), preferred_element_type=F32)
    o_ref[...] = epilogue(acc, pl.program_id(0), *refs[:-1]).astype(o_ref.dtype)


def _project(x, w_t, row0, n, name, epilogue=_epi_none, extra=(), extra_specs=(), out_dtype=F32, tm=1024, tn=1024):
    m, kd = x.shape
    tm, tn = min(tm, m), min(tn, n)
    align = 2 * SUBLANES
    assert m % tm == 0 and n % tn == 0 and row0 + n <= w_t.shape[0] and row0 % align == 0 and tn % align == 0
    return pl.pallas_call(
        functools.partial(_mm_kernel, epilogue=epilogue),
        grid=(n // tn, m // tm),
        in_specs=[pl.BlockSpec((tm, kd), lambda j, i: (i, 0)),
                  pl.BlockSpec((pl.Element(tn), pl.Element(kd)),
                               lambda j, i: (pl.multiple_of(row0 + j * tn, align), 0)), *extra_specs],
        out_specs=pl.BlockSpec((tm, tn), lambda j, i: (i, j)),
        out_shape=jax.ShapeDtypeStruct((m, n), out_dtype),
        compiler_params=_params(("parallel", "parallel")),
        name=name,
    )(x, w_t, *extra)


def _conv_kernel(u_ref, w_ref, o_ref, *, seq, tile, n_q, n_k):
    j = pl.program_id(0)
    n_tiles = seq // tile
    w = w_ref[...]
    pad = (CONV_WIDTH - 1) // 2
    ext_rows = tile + 2 * SUBLANES

    def finish(t0, acc):
        y = _silu(acc)
        inv = lax.rsqrt(jnp.sum(y * y, axis=-1, keepdims=True) + RMS_EPS)
        scale = jnp.where(j < n_q, inv * Q_SCALE, jnp.where(j < n_q + n_k, inv, 1.0))
        o_ref[pl.ds(t0, tile), :] = y * scale

    def edge(i):
        t0 = i * tile
        zeros = jnp.zeros((SUBLANES, LANES), F32)
        prev = u_ref[t0 - SUBLANES:t0, :] if i > 0 else zeros
        nxt = u_ref[t0 + tile:t0 + tile + SUBLANES, :] if i < n_tiles - 1 else zeros
        ext = jnp.concatenate([prev, u_ref[t0:t0 + tile, :], nxt], axis=0)
        acc = jnp.zeros((tile, LANES), F32)
        for tap in range(CONV_WIDTH):
            shift = (pad - tap) % ext_rows
            sh = ext if shift == 0 else pltpu.roll(ext, shift, axis=0)
            acc = acc + sh[SUBLANES:SUBLANES + tile] * w[tap:tap + 1, :]
        finish(t0, acc)

    def body(i, carry):
        t0 = pl.multiple_of(i * tile, tile)
        acc = jnp.zeros((tile, LANES), F32)
        for tap in range(CONV_WIDTH):
            acc = acc + u_ref[pl.ds(t0 + (tap - pad), tile), :] * w[tap:tap + 1, :]
        finish(t0, acc)
        return carry

    edge(0)
    if n_tiles > 1:
        lax.fori_loop(1, n_tiles - 1, body, 0)
        edge(n_tiles - 1)


def _conv_norm(u, conv_w8):
    seq, chans = u.shape
    tile = min(256, seq)
    return pl.pallas_call(
        functools.partial(_conv_kernel, seq=seq, tile=tile, n_q=GDN_K_HEADS, n_k=GDN_K_HEADS),
        grid=(chans // LANES,),
        in_specs=[pl.BlockSpec((seq, LANES), lambda j: (0, j)),
                  pl.BlockSpec((SUBLANES, LANES), lambda j: (0, j))],
        out_specs=pl.BlockSpec((seq, LANES), lambda j: (0, j)),
        out_shape=jax.ShapeDtypeStruct((seq, chans), F32),
        compiler_params=_params(("parallel",)),
        name="gdn_conv_norm",
    )(u, conv_w8)


GDN_SUB = 16


def _lane_block_diag(x, lane_block):
    xb = x.astype(BF16)
    zero = jnp.zeros_like(xb)
    n_blocks = x.shape[1] // x.shape[0]
    return jnp.concatenate([jnp.where(lane_block == r, xb, zero) for r in range(n_blocks)], axis=0)


def _packed_dot(x, y, lane_block):
    return jnp.dot(x.astype(BF16), _lane_block_diag(y, lane_block), preferred_element_type=F32)


def _unit_tri_inverse_minus_eye(a_list, same_block, lane_block):
    dot = functools.partial(_packed_dot, lane_block=lane_block)
    d = [jnp.where(same_block, a, 0.0) for a in a_list]
    off = [a - x for a, x in zip(a_list, d)]
    p = [-x for x in d]
    e = d
    for _ in range(3):
        e = [dot(x, x) for x in e]
        p = [x + y + dot(x, y) for x, y in zip(p, e)]
    m = [y + dot(x, y) for x, y in zip(p, off)]
    m2 = [dot(x, x) for x in m]
    q = [y - x - dot(x, y) for x, y in zip(m, m2)]
    return [x + y + dot(x, y) for x, y in zip(q, p)]


def _dot_split(a, b_bf):
    hi = a.astype(BF16)
    lo = (a - hi.astype(F32)).astype(BF16)
    return jnp.dot(hi, b_bf, preferred_element_type=F32) + jnp.dot(lo, b_bf, preferred_element_type=F32)


def _split_dot(a_bf, b):
    hi = b.astype(BF16)
    lo = (b - hi.astype(F32)).astype(BF16)
    return jnp.dot(a_bf, hi, preferred_element_type=F32) + jnp.dot(a_bf, lo, preferred_element_type=F32)


def _gdn_kernel(qf_ref, kf_ref, vf_ref, gf_ref, qb_ref, kb_ref, vb_ref, gb_ref,
                of_ref, ob_ref, st_ref, lhs_ref, psi_ref, oin_ref, *, n_chunks, chunk, n_kh):
    j = pl.program_id(0)

    @pl.when(pl.program_id(1) == 0)
    def _():
        st_ref[...] = jnp.zeros_like(st_ref)

    n_e = 2
    width = n_e * chunk
    row = lax.broadcasted_iota(jnp.int32, (chunk, width), 0)
    lane = lax.broadcasted_iota(jnp.int32, (chunk, width), 1)
    col = lane % chunk
    left = lane < chunk
    same_block = (row // GDN_SUB) == (col // GDN_SUB)
    r2 = lax.broadcasted_iota(jnp.int32, (chunk, chunk), 0)
    c2 = lax.broadcasted_iota(jnp.int32, (chunk, chunk), 1)
    sel_row = lax.broadcasted_iota(jnp.int32, (LANES, LANES), 0)
    sel_lane = lax.broadcasted_iota(jnp.int32, (LANES, LANES), 1)
    dirs = ((qf_ref, kf_ref, vf_ref, gf_ref, of_ref), (qb_ref, kb_ref, vb_ref, gb_ref, ob_ref))

    a_list, qkd_list, rhs_list, tail_list, k_list, gamma_list, qdec_list = [], [], [], [], [], [], []
    for kh, (d, (q_ref, k_ref, v_ref, g_ref, _)) in itertools.product(range(n_kh), enumerate(dirs)):
        backward = d == 1
        incl = (col >= row) if backward else (col <= row)
        strict = (col > row) if backward else (col < row)
        tri_c = ((c2 >= r2) if backward else (c2 <= r2)).astype(BF16)
        tri_r = ((row >= col) if backward else (row <= col)).astype(F32)
        last = 0 if backward else chunk - 1
        head0 = n_e * (n_kh * j + kh)
        src = d * GDN_V_HEADS + head0 + jnp.where(sel_lane < n_e, sel_lane, 2 * GDN_V_HEADS + sel_lane - n_e)
        sel = ((sel_row == src) & (sel_lane < 2 * n_e)).astype(BF16)
        picked_all = _dot_split(g_ref[...], sel)
        for c in range(n_chunks):
            rows = slice(c * chunk, (c + 1) * chunk)
            q = q_ref[rows, kh * HEAD_DIM:(kh + 1) * HEAD_DIM]
            k = k_ref[rows, kh * HEAD_DIM:(kh + 1) * HEAD_DIM]
            picked = picked_all[rows]
            cum = _split_dot(tri_c, picked)
            rep = lambda t, i: jnp.broadcast_to(t[:, i:i + 1], (chunk, LANES))
            g_rep = [rep(picked, e) for e in range(n_e)]
            b_rep = [rep(picked, n_e + e) for e in range(n_e)]
            cum_rep = [rep(cum, e) for e in range(n_e)]
            g_pair = jnp.where(left, g_rep[0], g_rep[1])
            cum_col = jnp.where(left, cum_rep[0], cum_rep[1])
            cum_row = jnp.sum(g_pair * tri_r, axis=0, keepdims=True)
            decay = jnp.where(incl, jnp.exp(jnp.where(incl, cum_col - cum_row, 0.0)), 0.0)
            k2 = jnp.concatenate([k, k], axis=0)
            kk = _dot_nt(k, k2)
            qk = _dot_nt(q, k2)
            a_list.append(jnp.where(strict, kk * decay, 0.0) * jnp.where(left, b_rep[0], b_rep[1]))
            qkd_list.append(qk * decay)
            k_list.append(k)
            rhs, tail = [], []
            for e in range(n_e):
                v = v_ref[rows, (kh * n_e + e) * HEAD_DIM:(kh * n_e + e + 1) * HEAD_DIM]
                g_last = cum_rep[e][last:last + 1, :]
                eg = jnp.exp(cum_rep[e])
                rhs.append(jnp.concatenate([v * b_rep[e], k * (b_rep[e] * eg)], axis=1))
                tail.append(jnp.exp(g_last - cum_rep[e]))
                gamma_list.append(jnp.exp(g_last))
                qdec_list.append(q * eg)
            rhs_list.append(jnp.concatenate(rhs, axis=0))
            tail_list.append(jnp.concatenate(tail, axis=0))

    lane_block = lane // chunk
    t_list = _unit_tri_inverse_minus_eye(a_list, same_block, lane_block)
    uw_list = [r + jnp.dot(_lane_block_diag(t, lane_block), r.astype(BF16), preferred_element_type=F32)
               for t, r in zip(t_list, rhs_list)]
    qo_list = [jnp.dot(_lane_block_diag(x, lane_block), y.astype(BF16), preferred_element_type=F32)
               for x, y in zip(qkd_list, uw_list)]
    for dc in range(n_kh * 2 * n_chunks):
        uw, tail = uw_list[dc], tail_list[dc]
        scaled = jnp.concatenate([uw[e * chunk:(e + 1) * chunk, h * HEAD_DIM:(h + 1) * HEAD_DIM]
                                  * tail[e * chunk:(e + 1) * chunk] for e in range(n_e) for h in range(2)],
                                 axis=1)
        kt = _dot_tn(k_list[dc], scaled)
        for e in range(n_e):
            b = n_e * dc + e
            base = 2 * HEAD_DIM * e
            qo = qo_list[dc][e * chunk:(e + 1) * chunk]
            psi_ref[b] = kt[:, base:base + HEAD_DIM]
            lhs_ref[b, :HEAD_DIM, :] = (-kt[:, base + HEAD_DIM:base + 2 * HEAD_DIM]).astype(BF16)
            lhs_ref[b, HEAD_DIM:, :] = (qdec_list[b] - qo[:, HEAD_DIM:]).astype(BF16)
            oin_ref[b] = qo[:, :HEAD_DIM]

    n_states = n_kh * 2 * n_e
    states = [st_ref[i] for i in range(n_states)]
    for step in range(n_chunks):
        for kh, d, e in itertools.product(range(n_kh), range(2), range(n_e)):
            c = n_chunks - 1 - step if d == 1 else step
            o_ref = dirs[d][4]
            b = n_e * ((kh * 2 + d) * n_chunks + c) + e
            i = (kh * 2 + d) * n_e + e
            s = states[i]
            res = jnp.dot(lhs_ref[b], s.astype(BF16), preferred_element_type=F32)
            states[i] = gamma_list[b] * s + res[:HEAD_DIM] + psi_ref[b]
            lanes = slice((kh * n_e + e) * HEAD_DIM, (kh * n_e + e + 1) * HEAD_DIM)
            o_ref[c * chunk:(c + 1) * chunk, lanes] = (res[HEAD_DIM:] + oin_ref[b]).astype(o_ref.dtype)
    for i in range(n_states):
        st_ref[i] = states[i]


def _gated_delta(qkv, gates, n_chunks=2, n_kh=4):
    seq = qkv.shape[0]
    blk = n_chunks * GDN_CHUNK
    nb = seq // blk
    rep = GDN_V_HEADS // GDN_K_HEADS
    kw = n_kh * HEAD_DIM
    vw = n_kh * rep * HEAD_DIM
    groups = GDN_K_HEADS // n_kh
    v_off = 2 * GDN_K_HEADS * HEAD_DIM // vw
    n_chains = n_kh * 2 * n_chunks * rep

    def specs(rowmap):
        return [pl.BlockSpec((blk, kw), lambda j, n: (rowmap(n), j)),
                pl.BlockSpec((blk, kw), lambda j, n: (rowmap(n), groups + j)),
                pl.BlockSpec((blk, vw), lambda j, n: (rowmap(n), v_off + j)),
                pl.BlockSpec((blk, LANES), lambda j, n: (rowmap(n), 0))]

    fwd = lambda n: n
    bwd = lambda n: nb - 1 - n
    out_sd = jax.ShapeDtypeStruct((seq, GDN_V_HEADS * HEAD_DIM), BF16)
    return pl.pallas_call(
        functools.partial(_gdn_kernel, n_chunks=n_chunks, chunk=GDN_CHUNK, n_kh=n_kh),
        grid=(groups, nb),
        in_specs=specs(fwd) + specs(bwd),
        out_specs=[pl.BlockSpec((blk, vw), lambda j, n: (fwd(n), j)),
                   pl.BlockSpec((blk, vw), lambda j, n: (bwd(n), j))],
        out_shape=[out_sd, out_sd],
        scratch_shapes=[pltpu.VMEM((n_kh * 2 * rep, HEAD_DIM, HEAD_DIM), F32),
                        pltpu.VMEM((n_chains, HEAD_DIM + GDN_CHUNK, HEAD_DIM), BF16),
                        pltpu.VMEM((n_chains, HEAD_DIM, HEAD_DIM), F32),
                        pltpu.VMEM((n_chains, GDN_CHUNK, HEAD_DIM), F32)],
        compiler_params=_params(("parallel", "arbitrary")),
        name="gdn_delta_rule",
    )(qkv, qkv, qkv, gates, qkv, qkv, qkv, gates)


ATT_TQ = 256
ATT_TILING = {16: (256, 4), 4: (128, 8), 1: (128, 8)}
ATT_OFFSETS = 3


def _att_kernel(q_ref, k_ref, v_ref, z_ref, o_ref, acc_ref, m_ref, l_ref, band_ref, *, seq):
    rel = (lax.broadcasted_iota(jnp.int32, band_ref.shape[1:], 0)
           - lax.broadcasted_iota(jnp.int32, band_ref.shape[1:], 1))
    for i in range(ATT_OFFSETS):
        band_ref[i] = (jnp.abs(rel + i * ATT_RADIUS) <= ATT_RADIUS).astype(F32)

    for p, dil in enumerate(DILATIONS):
        length = seq // dil
        tq = min(ATT_TILING[dil][0], length)
        win = min(tq + 2 * ATT_RADIUS, length)
        tiles = length // tq
        first, final = p == 0, p == len(DILATIONS) - 1
        n_it = dil * tiles
        per_trip = max(t for t in (8, 4, 2, 1) if n_it % t == 0 and t <= ATT_TILING[dil][1])

        def scores(it, dil=dil, tq=tq, win=win, tiles=tiles, length=length):
            r = it // tiles
            m0 = (it % tiles) * tq
            k0 = jnp.clip(m0 - ATT_RADIUS, 0, length - win)
            if dil == 1:
                qsl = pl.ds(pl.multiple_of(m0, tq), tq)
                ksl = pl.ds(pl.multiple_of(k0, ATT_RADIUS), win)
            else:
                qsl = pl.ds(r + dil * m0, tq, stride=dil)
                ksl = pl.ds(r + dil * k0, win, stride=dil)
            s = _dot_nt(q_ref[qsl, :], k_ref[ksl, :])
            if win == tq + 2 * ATT_RADIUS:
                valid = band_ref[(m0 - k0) // ATT_RADIUS, :tq, :win] > 0.0
            else:
                qpos = m0 + lax.broadcasted_iota(jnp.int32, (tq, win), 0)
                kpos = k0 + lax.broadcasted_iota(jnp.int32, (tq, win), 1)
                valid = jnp.abs(qpos - kpos) <= ATT_RADIUS
            return qsl, ksl, jnp.where(valid, s, NEG_INF)

        def tile(qsl, ksl, s, tq=tq, win=win, first=first, final=final):
            slabs = [s[:, c:min(c + LANES, win)] for c in range(0, win, LANES)]
            m_new = jnp.broadcast_to(jnp.max(functools.reduce(jnp.maximum, slabs), axis=-1, keepdims=True),
                                     (tq, LANES))
            if not first:
                m_old = m_ref[qsl, :]
                m_new = jnp.maximum(m_old, m_new)
            e_slabs = [jnp.exp(sl - m_new[:, :sl.shape[1]]) for sl in slabs]
            l_new = jnp.broadcast_to(jnp.sum(functools.reduce(jnp.add, e_slabs), axis=-1, keepdims=True),
                                     (tq, LANES))
            acc = _dot(jnp.concatenate(e_slabs, axis=1), v_ref[ksl, :])
            if not first:
                corr = jnp.exp(m_old - m_new)
                l_new = l_new + corr * l_ref[qsl, :]
                acc = acc + corr * acc_ref[qsl, :]
            if final:
                return qsl, (acc / l_new * _silu(z_ref[qsl, :].astype(F32))).astype(o_ref.dtype), None, None
            return qsl, acc, m_new, l_new

        def body(trip, carry, per_trip=per_trip, final=final, scores=scores, tile=tile):
            staged = [scores(trip * per_trip + t) for t in range(per_trip)]
            done = [tile(*st) for st in staged]
            for qsl, main, m_new, l_new in done:
                if final:
                    o_ref[qsl, :] = main
                else:
                    acc_ref[qsl, :] = main
                    m_ref[qsl, :] = m_new
                    l_ref[qsl, :] = l_new
            return carry

        lax.fori_loop(0, n_it // per_trip, body, 0)


def _dilated_attention(qk, v, z):
    seq, width = v.shape
    heads = width // HEAD_DIM
    col = pl.BlockSpec((seq, HEAD_DIM), lambda h: (0, h))
    return pl.pallas_call(
        functools.partial(_att_kernel, seq=seq),
        grid=(heads,),
        in_specs=[col, pl.BlockSpec((seq, HEAD_DIM), lambda h: (0, heads + h)), col, col],
        out_specs=col,
        out_shape=jax.ShapeDtypeStruct((seq, width), BF16),
        scratch_shapes=[pltpu.VMEM((seq, HEAD_DIM), F32), pltpu.VMEM((seq, LANES), F32),
                        pltpu.VMEM((seq, LANES), F32),
                        pltpu.VMEM((ATT_OFFSETS, ATT_TQ, ATT_TQ + 2 * ATT_RADIUS), F32)],
        compiler_params=_params(("parallel",)),
        name="dilated_attention",
    )(qk, qk, v, z)


def _gdn_out_kernel(of_ref, ob_ref, z_ref, w_ref, y_ref):
    w = w_ref[...]
    for h in range(of_ref.shape[1] // HEAD_DIM):
        sl = slice(h * HEAD_DIM, (h + 1) * HEAD_DIM)
        o = of_ref[:, sl].astype(F32) + ob_ref[:, sl].astype(F32)
        inv = lax.rsqrt(jnp.mean(o * o, axis=-1, keepdims=True) + RMS_EPS)
        y_ref[:, sl] = (o * inv * w * z_ref[:, sl].astype(F32)).astype(y_ref.dtype)


def _gdn_out(o_f, o_b, z_silu, norm_w):
    seq, width = o_f.shape
    tm = min(512, seq)
    blk = pl.BlockSpec((tm, width), lambda i: (i, 0))
    return pl.pallas_call(
        _gdn_out_kernel,
        grid=(seq // tm,),
        in_specs=[blk, blk, blk, pl.BlockSpec((1, HEAD_DIM), lambda i: (0, 0))],
        out_specs=blk,
        out_shape=jax.ShapeDtypeStruct((seq, width), BF16),
        compiler_params=_params(("parallel",)),
        name="gdn_norm_gate",
    )(o_f, o_b, z_silu, norm_w)


def _layer_norm_rows(pre_ref, g_ref, b_ref, out_refs, n_col, tn):
    width = n_col * tn
    total = jnp.zeros((pre_ref.shape[1], 1), F32)
    for c in range(n_col):
        total = total + jnp.sum(pre_ref[c], axis=-1, keepdims=True)
    mu = total / width
    sq = jnp.zeros_like(total)
    for c in range(n_col):
        dev = pre_ref[c] - mu
        sq = sq + jnp.sum(dev * dev, axis=-1, keepdims=True)
    inv = lax.rsqrt(sq / width + LN_EPS)
    for c in range(n_col):
        sl = slice(c * tn, (c + 1) * tn)
        y = (pre_ref[c] - mu) * inv * g_ref[:, sl] + b_ref[:, sl]
        for ref in out_refs:
            ref[:, sl] = y.astype(ref.dtype)


def _outproj_kernel(ya_ref, yb_ref, wa_ref, wb_ref, x_ref, g_ref, b_ref, h_ref, pre_ref, *, n_col, tn, alpha):
    n = pl.program_id(1)
    mix = (jnp.dot(ya_ref[...], wa_ref[...], preferred_element_type=F32)
           + jnp.dot(yb_ref[...], wb_ref[...], preferred_element_type=F32))
    pre_ref[n] = alpha * x_ref[...] + mix

    @pl.when(n == n_col - 1)
    def _():
        _layer_norm_rows(pre_ref, g_ref, b_ref, (h_ref,), n_col, tn)


def _out_projection(y_a, y_b, w, x, gamma, beta, alpha, tm=512, tn=512):
    seq, d_model = x.shape
    tm = min(tm, seq)
    n_col = d_model // tn
    ka, kb = y_a.shape[1], y_b.shape[1]
    assert ka == kb and w.shape[0] == ka + kb
    row = lambda i, n: (i, 0)
    full = pl.BlockSpec((tm, d_model), row)
    vec = pl.BlockSpec((1, d_model), lambda i, n: (0, 0))
    return pl.pallas_call(
        functools.partial(_outproj_kernel, n_col=n_col, tn=tn, alpha=alpha),
        grid=(seq // tm, n_col),
        in_specs=[pl.BlockSpec((tm, ka), row), pl.BlockSpec((tm, kb), row),
                  pl.BlockSpec((ka, tn), lambda i, n: (0, n)), pl.BlockSpec((kb, tn), lambda i, n: (1, n)),
                  pl.BlockSpec((tm, tn), lambda i, n: (i, n)), vec, vec],
        out_specs=full,
        out_shape=jax.ShapeDtypeStruct((seq, d_model), F32),
        scratch_shapes=[pltpu.VMEM((n_col, tm, tn), F32)],
        compiler_params=_params(("parallel", "arbitrary")),
        name="out_proj_ln",
    )(y_a, y_b, w, w, x, gamma, beta)


def _ple_kernel(hb_ref, wg_ref, p_ref, wu_ref, h_ref, g_ref, b_ref, o_ref, pre_ref, *, n_col, tn, alpha):
    n = pl.program_id(1)
    gate = _sigmoid(jnp.dot(hb_ref[...], wg_ref[...], preferred_element_type=F32))
    up = jnp.dot(p_ref[...], wu_ref[...], preferred_element_type=F32)
    pre_ref[n] = alpha * h_ref[...] + gate * up

    @pl.when(n == n_col - 1)
    def _():
        _layer_norm_rows(pre_ref, g_ref, b_ref, (o_ref,), n_col, tn)


def _ple_layer(h_bf, w_gate, p_bf, w_up, h, gamma, beta, alpha, tm=512, tn=512):
    seq, d_model = h.shape
    tm = min(tm, seq)
    n_col = d_model // tn
    pd = p_bf.shape[1]
    row = lambda i, n: (i, 0)
    vec = pl.BlockSpec((1, d_model), lambda i, n: (0, 0))
    return pl.pallas_call(
        functools.partial(_ple_kernel, n_col=n_col, tn=tn, alpha=alpha),
        grid=(seq // tm, n_col),
        in_specs=[pl.BlockSpec((tm, d_model), row), pl.BlockSpec((d_model, tn), lambda i, n: (0, n)),
                  pl.BlockSpec((tm, pd), row), pl.BlockSpec((pd, tn), lambda i, n: (0, n)),
                  pl.BlockSpec((tm, tn), lambda i, n: (i, n)), vec, vec],
        out_specs=pl.BlockSpec((tm, d_model), row),
        out_shape=jax.ShapeDtypeStruct((seq, d_model), F32),
        scratch_shapes=[pltpu.VMEM((n_col, tm, tn), F32)],
        compiler_params=_params(("parallel", "arbitrary")),
        name="ple_ln",
    )(h_bf, w_gate, p_bf, w_up, h, gamma, beta)


def _rotary_tables(seq):
    half = HEAD_DIM // 2
    inv_freq = 1.0 / (jnp.float32(ROPE_THETA) ** (jnp.arange(half, dtype=F32) / half))
    ang = jnp.arange(seq).astype(F32)[:, None] * inv_freq[None, :]
    cos, sin = jnp.cos(ang), jnp.sin(ang)
    return jnp.concatenate([cos, cos], axis=-1), jnp.concatenate([-sin, sin], axis=-1)


def _layer(h, p, w_in, conv_w, a_log, dt_bias, gdn_norm_w, w_out, ln1_g, ln1_b, w_ple_gate, w_ple_up,
           ln2_g, ln2_b, alpha):
    seq, d_model = h.shape
    key_dim = GDN_K_HEADS * HEAD_DIM
    d_gdn = GDN_V_HEADS * HEAD_DIM
    d_att = ATT_HEADS * HEAD_DIM
    n_gate = 2 * GDN_V_HEADS
    o_qkv = 0
    o_za = 2 * key_dim + d_gdn
    o_ab = o_za + d_gdn
    o_qk = o_ab + 2 * n_gate
    o_vb = o_qk + 2 * d_att
    o_zb = o_vb + d_att

    x_bf = h.astype(BF16)
    w_t = jnp.swapaxes(w_in, 0, 1)
    w_bf = w_t.astype(BF16)

    qkv_a = _project(x_bf, w_bf, o_qkv, o_za - o_qkv, "proj_gdn_qkv")
    za = _project(x_bf, w_bf, o_za, d_gdn, "proj_gdn_gate", _epi_silu, out_dtype=BF16)
    w_ab = jnp.pad(w_bf[o_ab:o_qk], ((0, LANES - 2 * n_gate), (0, 0)))
    pad_vec = lambda t: jnp.pad(t.reshape(1, n_gate).astype(F32), ((0, 0), (0, LANES - n_gate)))
    vec_spec = pl.BlockSpec((1, LANES), lambda j, i: (0, 0))
    gates = _project(x_bf, w_ab, 0, LANES, "proj_gdn_decay", _epi_gates,
                     extra=(pad_vec(a_log), pad_vec(dt_bias)), extra_specs=(vec_spec, vec_spec))
    cos, sin = _rotary_tables(seq)
    tm_rot = min(1024, seq)
    tab_spec = pl.BlockSpec((tm_rot, HEAD_DIM), lambda j, i: (i, 0))
    tn_rot = 1024
    qk_b = _project(x_bf, w_bf, o_qk, 2 * d_att, "proj_att_qk",
                    functools.partial(_epi_rotary, n_q_blocks=d_att // tn_rot),
                    extra=(cos, sin), extra_specs=(tab_spec, tab_spec), tm=tm_rot, tn=tn_rot)
    v_b = _project(x_bf, w_bf, o_vb, d_att, "proj_att_v")
    z_b = _project(x_bf, w_bf, o_zb, d_att, "proj_att_gate", out_dtype=BF16)

    conv_w8 = jnp.pad(conv_w.astype(F32), ((0, SUBLANES - CONV_WIDTH), (0, 0)))
    qkv_n = _conv_norm(qkv_a, conv_w8)

    o_f, o_b = _gated_delta(qkv_n, gates)

    y_gdn = _gdn_out(o_f, o_b, za, gdn_norm_w.reshape(1, HEAD_DIM).astype(F32))

    y_att = _dilated_attention(qk_b, v_b, z_b)

    h1 = _out_projection(y_gdn, y_att, w_out.astype(BF16), h,
                         ln1_g.reshape(1, d_model), ln1_b.reshape(1, d_model), alpha)

    return _ple_layer(h1.astype(BF16), w_ple_gate.astype(BF16), p.astype(BF16), w_ple_up.astype(BF16), h1,
                      ln2_g.reshape(1, d_model), ln2_b.reshape(1, d_model), alpha)


def kernel(x, p, w_in, conv_w, a_log, dt_bias, gdn_norm_w, w_out, ln1_g, ln1_b, w_ple_gate, w_ple_up, ln2_g, ln2_b):
    batch = x.shape[0]
    depth = w_in.shape[0]
    alpha = (2 * depth) ** 0.25
    outs = []
    for b in range(batch):
        h = x[b]
        for i in range(depth):
            h = _layer(h, p[i, b], w_in[i], conv_w[i], a_log[i], dt_bias[i], gdn_norm_w[i], w_out[i],
                       ln1_g[i], ln1_b[i], w_ple_gate[i], w_ple_up[i], ln2_g[i], ln2_b[i], alpha)
        outs.append(h)
    return jnp.stack(outs)
```
